```python
import math
import jax, jax.numpy as jnp
from jax import lax
import numpy as np

D_MODEL = 1024
BATCH = 4
SEQ = 8192
DEPTH = 1

ATT_WIDTH = D_MODEL // 2
ATT_HEAD_DIM = 64
ATT_HEADS = ATT_WIDTH // ATT_HEAD_DIM
MOBA_BLOCK = 256
MOBA_TOPK = 3
MOBA_QCHUNK = 64
SSM_WIDTH = D_MODEL // 2
SSM_GROUP = 16
SSM_GROUPS = SSM_WIDTH // SSM_GROUP
SSM_STATE = 64
IN_WIDTH = 3 * ATT_WIDTH + SSM_WIDTH + 2 * D_MODEL
MEM_LEN = 256
XATTN_HEADS = 4
XATTN_HEAD_DIM = D_MODEL // 8
XATTN_WIDTH = XATTN_HEADS * XATTN_HEAD_DIM
N_EXPERTS = 32
EXPERT_TOPK = 4
D_FF = D_MODEL
SWIGLU_LIMIT = 7.0
SWIGLU_ALPHA = 1.702
MOE_BLOCK = 256
RMS_EPS = 1e-6
NEG_INF = -1e30

kernel_name = "hybrid_moba_s5_xattn_moe_layer"


def rmsnorm(x, g):
    xf = x.astype(jnp.float32)
    y = xf * lax.rsqrt(jnp.mean(xf * xf, axis=-1, keepdims=True) + RMS_EPS)
    return (y * g.astype(jnp.float32)).astype(x.dtype)


def moba_attention(q, k, v):
    B, H, S, Dh = q.shape
    nb = -(-S // MOBA_BLOCK)
    sp = nb * MOBA_BLOCK
    pad = ((0, 0), (0, 0), (0, sp - S), (0, 0))
    q, k, v = jnp.pad(q, pad), jnp.pad(k, pad), jnp.pad(v, pad)
    kb = k.reshape(B, H, nb, MOBA_BLOCK, Dh)
    vb = v.reshape(B, H, nb, MOBA_BLOCK, Dh)
    kmean = jnp.mean(kb.astype(jnp.float32), axis=3)
    topk = min(MOBA_TOPK, nb)
    nc = sp // MOBA_QCHUNK
    qc = q.reshape(B, H, nc, MOBA_QCHUNK, Dh).transpose(2, 0, 1, 3, 4)
    scale = Dh ** -0.5
    gather_blocks = jax.vmap(jax.vmap(lambda blocks, idx: blocks[idx]))

    def chunk(args):
        qi, ci = args
        qf = qi.astype(jnp.float32)
        start = ci * MOBA_QCHUNK
        own = start // MOBA_BLOCK
        qpos = start + jnp.arange(MOBA_QCHUNK)
        gate = jnp.einsum('bhqd,bhnd->bhqn', qf, kmean)
        gate = jnp.where(jnp.arange(nb) < own, gate, NEG_INF)
        _, sel = lax.top_k(gate, topk)
        valid = sel < own
        ks = gather_blocks(kb, sel)
        vs = gather_blocks(vb, sel)
        s_sel = jnp.einsum('bhqd,bhqkjd->bhqkj', qf, ks.astype(jnp.float32)) * scale
        s_sel = jnp.where(valid[..., None], s_sel, NEG_INF).reshape(B, H, MOBA_QCHUNK, topk * MOBA_BLOCK)
        k_own = lax.dynamic_index_in_dim(kb, own, axis=2, keepdims=False)
        v_own = lax.dynamic_index_in_dim(vb, own, axis=2, keepdims=False)
        s_own = jnp.einsum('bhqd,bhjd->bhqj', qf, k_own.astype(jnp.float32)) * scale
        kpos = own * MOBA_BLOCK + jnp.arange(MOBA_BLOCK)
        s_own = jnp.where(kpos[None, :] <= qpos[:, None], s_own, NEG_INF)
        p = jax.nn.softmax(jnp.concatenate([s_sel, s_own], axis=-1), axis=-1)
        p_sel = p[..., :topk * MOBA_BLOCK].reshape(B, H, MOBA_QCHUNK, topk, MOBA_BLOCK)
        p_own = p[..., topk * MOBA_BLOCK:]
        out = (jnp.einsum('bhqkj,bhqkjd->bhqd', p_sel, vs.astype(jnp.float32))
               + jnp.einsum('bhqj,bhjd->bhqd', p_own, v_own.astype(jnp.float32)))
        return out.astype(q.dtype)

    outs = lax.map(chunk, (qc, jnp.arange(nc)))
    return outs.transpose(1, 2, 0, 3, 4).reshape(B, H, sp, Dh)[:, :, :S]


def s5_ssm(u, a_re, a_im, log_dt, b_re, b_im, c_re, c_im, d_skip):
    B, S, G, C = u.shape
    uf = u.astype(jnp.float32)
    lam = lax.complex(jnp.minimum(a_re.astype(jnp.float32), -1e-4), a_im.astype(jnp.float32))
    dt = jnp.exp(log_dt.astype(jnp.float32))[:, None]
    lam_bar = jnp.exp(lam * dt)
    b_c = lax.complex(b_re.astype(jnp.float32), b_im.astype(jnp.float32))
    c_c = lax.complex(c_re.astype(jnp.float32), c_im.astype(jnp.float32))
    b_bar = ((lam_bar - 1.0) / lam)[..., None] * b_c
    bu = jnp.einsum('gpc,bsgc->sbgp', b_bar, uf.astype(jnp.complex64))
    a = jnp.broadcast_to(lam_bar[None, None], (S, 1, G, SSM_STATE))

    def combine(left, right):
        a1, b1 = left
        a2, b2 = right
        return a1 * a2, a2 * b1 + b2

    _, xs = lax.associative_scan(combine, (a, bu), axis=0)
    y = jnp.einsum('gcp,sbgp->bsgc', c_c, xs).real + d_skip.astype(jnp.float32) * uf
    return y.astype(u.dtype)


def hybrid_mixer(h, g, w_in, a_re, a_im, log_dt, b_re, b_im, c_re, c_im, d_skip,
                 w_glu, w_proj_a, w_out):
    B, S, _ = h.shape
    n = rmsnorm(h, g)
    z = n @ w_in
    splits = [ATT_WIDTH, 2 * ATT_WIDTH, 3 * ATT_WIDTH, 3 * ATT_WIDTH + SSM_WIDTH,
              3 * ATT_WIDTH + SSM_WIDTH + D_MODEL]
    q, k, v, u, ga, gb = jnp.split(z, splits, axis=-1)
    to_heads = lambda t: t.reshape(B, S, ATT_HEADS, ATT_HEAD_DIM).transpose(0, 2, 1, 3)
    ya = moba_attention(to_heads(q), to_heads(k), to_heads(v))
    ya = ya.transpose(0, 2, 1, 3).reshape(B, S, ATT_WIDTH) @ w_proj_a
    yb = s5_ssm(u.reshape(B, S, SSM_GROUPS, SSM_GROUP), a_re, a_im, log_dt,
                b_re, b_im, c_re, c_im, d_skip).reshape(B, S, SSM_WIDTH)
    zb = jax.nn.gelu(yb) @ w_glu
    yb = zb[..., :D_MODEL] * jax.nn.sigmoid(zb[..., D_MODEL:])
    merged = jax.nn.sigmoid(ga) * ya + jax.nn.sigmoid(gb) * yb
    return merged @ w_out


def cross_attention(h, mem, g, g_mem, w_xq, w_xkv, w_xo):
    B, S, _ = h.shape
    M = mem.shape[1]
    n = rmsnorm(h, g)
    mn = rmsnorm(mem, g_mem)
    q = (n @ w_xq).reshape(B, S, XATTN_HEADS, XATTN_HEAD_DIM)
    kv = mn @ w_xkv
    k = kv[..., :XATTN_WIDTH].reshape(B, M, XATTN_HEADS, XATTN_HEAD_DIM)
    v = kv[..., XATTN_WIDTH:].reshape(B, M, XATTN_HEADS, XATTN_HEAD_DIM)
    s = jnp.einsum('bshd,bmhd->bhsm', q, k).astype(jnp.float32) * (XATTN_HEAD_DIM ** -0.5)
    p = jax.nn.softmax(s, axis=-1).astype(v.dtype)
    o = jnp.einsum('bhsm,bmhd->bshd', p, v).reshape(B, S, XATTN_WIDTH)
    return o @ w_xo


def moe_ffn(x2, w_router, b_router, w_gu, b_gu, w_dn, b_dn):
    T, D = x2.shape
    logits = (x2 @ w_router + b_router).astype(jnp.float32)
    top_v, top_i = lax.top_k(logits, EXPERT_TOPK)
    gates = jax.nn.softmax(top_v, axis=-1)
    N = T * EXPERT_TOPK
    e_flat = top_i.reshape(N)
    order = jnp.argsort(e_flat)
    e_sorted = e_flat[order]
    tok_sorted = order // EXPERT_TOPK
    gate_sorted = gates.reshape(N)[order]
    counts = jnp.bincount(e_flat, length=N_EXPERTS)
    starts = jnp.cumsum(counts) - counts
    padded = ((counts + MOE_BLOCK - 1) // MOE_BLOCK) * MOE_BLOCK
    pends = jnp.cumsum(padded)
    pstarts = pends - padded
    dest = pstarts[e_sorted] + jnp.arange(N) - starts[e_sorted]
    nblk = -(-N // MOE_BLOCK) + N_EXPERTS
    buf = jnp.zeros((nblk * MOE_BLOCK, D), x2.dtype).at[dest].set(x2[tok_sorted])
    blk_expert = jnp.clip(jnp.searchsorted(pends, jnp.arange(nblk) * MOE_BLOCK, side='right'),
                          0, N_EXPERTS - 1)

    def expert_block(args):
        xb, e = args
        gu = xb @ w_gu[e] + b_gu[e]
        gate = jnp.minimum(gu[:, :D_FF], SWIGLU_LIMIT)
        up = jnp.clip(gu[:, D_FF:], -SWIGLU_LIMIT, SWIGLU_LIMIT)
        hid = (up + 1.0) * gate * jax.nn.sigmoid(SWIGLU_ALPHA * gate)
        return hid @ w_dn[e] + b_dn[e]

    out = lax.map(expert_block, (buf.reshape(nblk, MOE_BLOCK, D), blk_expert)).reshape(nblk * MOE_BLOCK, D)
    y = out[dest] * gate_sorted.astype(out.dtype)[:, None]
    return jax.ops.segment_sum(y, tok_sorted, num_segments=T)


def setup_inputs(seed: int = 0) -> dict:
    key = jax.random.key(seed)
    ks = jax.random.split(key, 32)
    L = DEPTH
    f32 = jnp.float32
    nrm = lambda k, shape, scale: jax.random.normal(k, shape, f32) * scale
    gain = lambda k, shape: 1.0 + 0.02 * jax.random.normal(k, shape, f32)
    return {
        "x": nrm(ks[0], (BATCH, SEQ, D_MODEL), 1.0),
        "mem": nrm(ks[1], (BATCH, MEM_LEN, D_MODEL), 1.0),
        "g_mix": gain(ks[2], (L, D_MODEL)),
        "w_in": nrm(ks[3], (L, D_MODEL, IN_WIDTH), D_MODEL ** -0.5),
        "a_re": -0.5 + nrm(ks[4], (L, SSM_GROUPS, SSM_STATE), 0.01),
        "a_im": math.pi * jnp.arange(SSM_STATE, dtype=f32)[None, None, :]
                + nrm(ks[5], (L, SSM_GROUPS, SSM_STATE), 0.01),
        "log_dt": jax.random.uniform(ks[6], (L, SSM_GROUPS), f32, math.log(1e-3), math.log(1e-1)),
        "b_re": nrm(ks[7], (L, SSM_GROUPS, SSM_STATE, SSM_GROUP), (2 * SSM_GROUP) ** -0.5),
        "b_im": nrm(ks[8], (L, SSM_GROUPS, SSM_STATE, SSM_GROUP), (2 * SSM_GROUP) ** -0.5),
        "c_re": nrm(ks[9], (L, SSM_GROUPS, SSM_GROUP, SSM_STATE), (2 * SSM_STATE) ** -0.5),
        "c_im": nrm(ks[10], (L, SSM_GROUPS, SSM_GROUP, SSM_STATE), (2 * SSM_STATE) ** -0.5),
        "d_skip": nrm(ks[11], (L, SSM_GROUPS, SSM_GROUP), 1.0),
        "w_glu": nrm(ks[12], (L, SSM_WIDTH, 2 * D_MODEL), SSM_WIDTH ** -0.5),
        "w_proj_a": nrm(ks[13], (L, ATT_WIDTH, D_MODEL), ATT_WIDTH ** -0.5),
        "w_out": nrm(ks[14], (L, D_MODEL, D_MODEL), D_MODEL ** -0.5),
        "g_xattn": gain(ks[15], (L, D_MODEL)),
        "g_mem": gain(ks[16], (L, D_MODEL)),
        "w_xq": nrm(ks[17], (L, D_MODEL, XATTN_WIDTH), D_MODEL ** -0.5),
        "w_xkv": nrm(ks[18], (L, D_MODEL, 2 * XATTN_WIDTH), D_MODEL ** -0.5),
        "w_xo": nrm(ks[19], (L, XATTN_WIDTH, D_MODEL), XATTN_WIDTH ** -0.5),
        "g_moe": gain(ks[20], (L, D_MODEL)),
        "w_router": nrm(ks[21], (L, D_MODEL, N_EXPERTS), D_MODEL ** -0.5),
        "b_router": nrm(ks[22], (L, N_EXPERTS), 0.01),
        "w_gu": nrm(ks[23], (L, N_EXPERTS, D_MODEL, 2 * D_FF), D_MODEL ** -0.5),
        "b_gu": nrm(ks[24], (L, N_EXPERTS, 2 * D_FF), 0.01),
        "w_dn": nrm(ks[25], (L, N_EXPERTS, D_FF, D_MODEL), D_FF ** -0.5),
        "b_dn": nrm(ks[26], (L, N_EXPERTS, D_MODEL), 0.01),
        "g_final": gain(ks[27], (D_MODEL,)),
    }


def reference(x, mem, g_mix, w_in, a_re, a_im, log_dt, b_re, b_im, c_re, c_im, d_skip,
              w_glu, w_proj_a, w_out, g_xattn, g_mem, w_xq, w_xkv, w_xo,
              g_moe, w_router, b_router, w_gu, b_gu, w_dn, b_dn, g_final):
    h = x
    for l in range(DEPTH):
        h = h + hybrid_mixer(h, g_mix[l], w_in[l], a_re[l], a_im[l], log_dt[l], b_re[l], b_im[l],
                             c_re[l], c_im[l], d_skip[l], w_glu[l], w_proj_a[l], w_out[l])
        h = h + cross_attention(h, mem, g_xattn[l], g_mem[l], w_xq[l], w_xkv[l], w_xo[l])
        n = rmsnorm(h, g_moe[l]).reshape(-1, D_MODEL)
        h = h + moe_ffn(n, w_router[l], b_router[l], w_gu[l], b_gu[l], w_dn[l], b_dn[l]).reshape(h.shape)
    return rmsnorm(h, g_final)
```

```python
import functools
import math

import jax
import jax.numpy as jnp
from jax import lax
from jax.experimental import pallas as pl
from jax.experimental.pallas import tpu as pltpu

F32 = jnp.float32
BF16 = jnp.bfloat16
I32 = jnp.int32

RMS_EPS = 1e-6
NEG_INF = -1e30
ATT_HEADS = 8
ATT_HEAD_DIM = 64
MOBA_BLOCK = 256
MOBA_TOPK = 3
SSM_GROUP = 16
SSM_STATE = 64
SSM_CHUNK = 128
XATTN_HEADS = 4
XATTN_HEAD_DIM = 128
N_EXPERTS = 32
EXPERT_TOPK = 4
SWIGLU_LIMIT = 7.0
SWIGLU_ALPHA = 1.702
MOE_BLOCK = 256
LANES = 128
VMEM_LIMIT = 56 * 1024 * 1024


def _cparams(sem):
    return pltpu.CompilerParams(dimension_semantics=sem, vmem_limit_bytes=VMEM_LIMIT)


def _rmsnorm(x, g):
    return x * lax.rsqrt(jnp.mean(x * x, axis=-1, keepdims=True) + RMS_EPS) * g


def _dot(a, b):
    return jnp.dot(a, b, preferred_element_type=F32)


def _dot_nt(a, b):
    return lax.dot_general(a, b, (((1,), (1,)), ((), ())), preferred_element_type=F32)


def _dot_tn(a, b):
    return lax.dot_general(a, b, (((0,), (0,)), ((), ())), preferred_element_type=F32)


def _split3(x):
    a = x.astype(BF16)
    r = x - a.astype(F32)
    b = r.astype(BF16)
    c = (r - b.astype(F32)).astype(BF16)
    return a, b, c


def _inproj_kernel(x_ref, g_ref, wg_ref, wqv_ref, wk_ref, zt_ref, q_ref, v_ref, k_ref, km_ref):
    n = _rmsnorm(x_ref[...], g_ref[...]).astype(BF16)
    tm = n.shape[0]
    nblk = tm // MOBA_BLOCK
    rows = 512
    for c in range(wg_ref.shape[0] // rows):
        zt_ref[c * rows:(c + 1) * rows, :] = _dot_nt(wg_ref[c * rows:(c + 1) * rows, :], n).astype(BF16)
    for c, dst in ((0, q_ref), (1, v_ref)):
        r = _dot_nt(wqv_ref[c * rows:(c + 1) * rows, :], n).astype(BF16)
        for h in range(ATT_HEADS):
            for j in range(nblk):
                dst[h, j] = r[h * ATT_HEAD_DIM:(h + 1) * ATT_HEAD_DIM, j * MOBA_BLOCK:(j + 1) * MOBA_BLOCK]
    k = _dot(n, wk_ref[...])
    k_ref[...] = k.astype(BF16)
    km_ref[0] = jnp.mean(k.reshape(nblk, MOBA_BLOCK, k.shape[1]), axis=1)


def _in_proj(x2, g, wg_t, wqv_t, wk_pad, tm=512):
    T, D = x2.shape
    nblk = tm // MOBA_BLOCK
    ng = wg_t.shape[0]
    kw = wk_pad.shape[1]
    hd = (ATT_HEADS, T // MOBA_BLOCK, ATT_HEAD_DIM, MOBA_BLOCK)
    return pl.pallas_call(
        _inproj_kernel,
        grid=(T // tm,),
        in_specs=[
            pl.BlockSpec((tm, D), lambda i: (i, 0)),
            pl.BlockSpec((1, D), lambda i: (0, 0)),
            pl.BlockSpec((ng, D), lambda i: (0, 0)),
            pl.BlockSpec(wqv_t.shape, lambda i: (0, 0)),
            pl.BlockSpec((D, kw), lambda i: (0, 0)),
        ],
        out_specs=[
            pl.BlockSpec((ng, tm), lambda i: (0, i)),
            pl.BlockSpec((ATT_HEADS, nblk, ATT_HEAD_DIM, MOBA_BLOCK), lambda i: (0, i, 0, 0)),
            pl.BlockSpec((ATT_HEADS, nblk, ATT_HEAD_DIM, MOBA_BLOCK), lambda i: (0, i, 0, 0)),
            pl.BlockSpec((tm, kw), lambda i: (i, 0)),
            pl.BlockSpec((1, nblk, kw), lambda i: (i, 0, 0)),
        ],
        out_shape=[
            jax.ShapeDtypeStruct((ng, T), BF16),
            jax.ShapeDtypeStruct(hd, BF16),
            jax.ShapeDtypeStruct(hd, BF16),
            jax.ShapeDtypeStruct((T, kw), BF16),
            jax.ShapeDtypeStruct((T // tm, nblk, kw), F32),
        ],
        compiler_params=_cparams(("arbitrary",)),
        name="in_proj",
    )(x2, g, wg_t, wqv_t, wk_pad)


def _moba_kernel(q_ref, v_ref, k_ref, km_ref, o_ref, sel_ref, *, nb):
    blk = MOBA_BLOCK
    km_a, km_b, km_c = _split3(km_ref[0])
    zpad = jnp.zeros((LANES - ATT_HEAD_DIM, blk), BF16)
    bidx = lax.broadcasted_iota(I32, (nb, blk), 0)
    causal = lax.broadcasted_iota(I32, (blk, blk), 0) <= lax.broadcasted_iota(I32, (blk, blk), 1)
    topk = min(MOBA_TOPK, nb)

    def qblock(i, carry):
        q = jnp.concatenate([q_ref[0, i], zpad], axis=0)
        gate = (_dot(km_a, q) + _dot(km_b, q)) + _dot(km_c, q)
        elig = bidx < i
        g = jnp.where(elig, gate, NEG_INF)
        sel = jnp.zeros((nb, blk), F32)
        for _ in range(topk):
            mx = jnp.max(g, axis=0, keepdims=True)
            first = jnp.min(jnp.where(g == mx, bidx, nb), axis=0, keepdims=True)
            pick = bidx == first
            sel = jnp.where(pick, 1.0, sel)
            g = jnp.where(pick, -3e38, g)
        sel_ref[...] = jnp.where(elig, sel, 0.0)

        s = jnp.where(causal, _dot(k_ref[i], q), NEG_INF)
        m = jnp.max(s, axis=0, keepdims=True)
        p = jnp.exp(s - m)
        l = jnp.sum(p, axis=0, keepdims=True)
        acc = _dot(v_ref[0, i], p.astype(BF16))

        def kvblock(j, c):
            m, l, acc = c
            picked = sel_ref[pl.ds(j, 1), :] > 0.5
            s = jnp.where(picked, _dot(k_ref[j], q), NEG_INF)
            m_new = jnp.maximum(m, jnp.max(s, axis=0, keepdims=True))
            alpha = jnp.exp(m - m_new)
            p = jnp.exp(s - m_new)
            l = l * alpha + jnp.sum(p, axis=0, keepdims=True)
            acc = acc * alpha + _dot(v_ref[0, j], p.astype(BF16))
            return m_new, l, acc

        m, l, acc = lax.fori_loop(0, i, kvblock, (m, l, acc))
        o_ref[0, i] = (acc / l).astype(BF16)
        return carry

    lax.fori_loop(0, nb, qblock, 0)


def _moba(q4, v4, k3, km3, batch):
    H, nblk_total, dh, blk = q4.shape
    nb = nblk_total // batch
    kw = k3.shape[2]
    return pl.pallas_call(
        functools.partial(_moba_kernel, nb=nb),
        grid=(batch, H),
        in_specs=[
            pl.BlockSpec((1, nb, dh, blk), lambda b, h: (h, b, 0, 0)),
            pl.BlockSpec((1, nb, dh, blk), lambda b, h: (h, b, 0, 0)),
            pl.BlockSpec((nb, blk, LANES), lambda b, h: (b, 0, h)),
            pl.BlockSpec((1, nb, LANES), lambda b, h: (b, 0, h)),
        ],
        out_specs=pl.BlockSpec((1, nb, dh, blk), lambda b, h: (h, b, 0, 0)),
        out_shape=jax.ShapeDtypeStruct(q4.shape, BF16),
        scratch_shapes=[pltpu.VMEM((nb, blk), F32)],
        compiler_params=_cparams(("arbitrary", "arbitrary")),
        name="moba",
    )(q4, v4, k3, km3)


def _ssm_kernel(u_ref, ar_ref, ai_ref, ldt_ref, arc_ref, aic_ref, btr_ref, bti_ref, cr_ref, ci_ref,
                ctr_ref, cti_ref, d_ref, y_ref,
                u2_ref, m_ref, pr_ref, pi_ref, qr_ref, qi_ref, kv_ref, sr_ref, si_ref, xr_ref, xi_ref,
                *, nbatch):
    L = SSM_CHUNK
    C = SSM_GROUP
    N = u_ref.shape[1]
    nchunk = N // nbatch
    hi = lax.Precision.HIGHEST

    dt = jnp.exp(ldt_ref[0])
    are = jnp.minimum(ar_ref[0], -1e-4)
    aim = ai_ref[0]
    ea, th = are * dt, aim * dt
    mag = jnp.exp(ea)
    lbr, lbi = mag * jnp.cos(th), mag * jnp.sin(th)
    den = are * are + aim * aim
    cfr = ((lbr - 1.0) * are + lbi * aim) / den
    cfi = (lbi * are - (lbr - 1.0) * aim) / den
    btr, bti = btr_ref[0], bti_ref[0]
    bbr = cfr * btr - cfi * bti
    bbi = cfr * bti + cfi * btr
    crr, cri = cr_ref[0], ci_ref[0]

    eac = jnp.minimum(arc_ref[0], -1e-4) * dt
    thc = aic_ref[0] * dt
    tau = lax.broadcasted_iota(I32, (1, L), 1).astype(F32)
    m0 = jnp.exp(eac * tau)
    pw0r, pw0i = m0 * jnp.cos(thc * tau), m0 * jnp.sin(thc * tau)
    m1 = jnp.exp(eac * (tau + 1.0))
    pw1r, pw1i = m1 * jnp.cos(thc * (tau + 1.0)), m1 * jnp.sin(thc * (tau + 1.0))
    back = (L - 1.0) - lax.broadcasted_iota(I32, (L, 1), 0).astype(F32)
    mb = jnp.exp(ea * back)
    pbr, pbi = mb * jnp.cos(th * back), mb * jnp.sin(th * back)
    ml = jnp.exp(ea * float(L))
    alr, ali = ml * jnp.cos(th * float(L)), ml * jnp.sin(th * float(L))

    upper = lax.broadcasted_iota(I32, (L, L), 1) >= lax.broadcasted_iota(I32, (L, L), 0)
    for ci in range(C):
        cbr = bbr[ci:ci + 1, :] * crr - bbi[ci:ci + 1, :] * cri
        cbi = bbr[ci:ci + 1, :] * cri + bbi[ci:ci + 1, :] * crr
        kv_ref[ci * C:(ci + 1) * C, :] = (jnp.dot(cbr, pw0r, precision=hi, preferred_element_type=F32)
                                          - jnp.dot(cbi, pw0i, precision=hi, preferred_element_type=F32))
        pr_ref[ci * L:(ci + 1) * L, :] = (pbr * bbr[ci:ci + 1, :] - pbi * bbi[ci:ci + 1, :]).astype(BF16)
        pi_ref[ci * L:(ci + 1) * L, :] = (pbr * bbi[ci:ci + 1, :] + pbi * bbr[ci:ci + 1, :]).astype(BF16)
        u2_ref[:, ci * L:(ci + 1) * L] = u_ref[ci]
    for co in range(C):
        cc_r, cc_i = ctr_ref[0, co], cti_ref[0, co]
        qr_ref[:, co * L:(co + 1) * L] = (cc_r * pw1r - cc_i * pw1i).astype(BF16)
        qi_ref[:, co * L:(co + 1) * L] = (-(cc_r * pw1i + cc_i * pw1r)).astype(BF16)

    def toeplitz(ci, carry):
        for co in range(C):
            taps = kv_ref[pl.ds(ci * C + co, 1), :]
            t = pltpu.roll(jnp.broadcast_to(taps, (L, L)), 0, 1, stride=1, stride_axis=0)
            m_ref[pl.ds(pl.multiple_of(ci * L, L), L), co * L:(co + 1) * L] = jnp.where(upper, t, 0.0).astype(BF16)
        return carry

    lax.fori_loop(0, C, toeplitz, 0)

    u2 = u2_ref[...]
    sr_ref[...] = _dot(u2, pr_ref[...]).reshape(nbatch, nchunk, LANES)
    si_ref[...] = _dot(u2, pi_ref[...]).reshape(nbatch, nchunk, LANES)

    def chunk_scan(c, carry):
        xr, xi = carry
        xr_ref[:, pl.ds(c, 1), :] = xr
        xi_ref[:, pl.ds(c, 1), :] = xi
        nr = alr * xr - ali * xi + sr_ref[:, pl.ds(c, 1), :]
        ni = alr * xi + ali * xr + si_ref[:, pl.ds(c, 1), :]
        return nr, ni

    zero = jnp.zeros((nbatch, 1, LANES), F32)
    lax.fori_loop(0, nchunk, chunk_scan, (zero, zero))
    xpr = xr_ref[...].reshape(N, LANES).astype(BF16)
    xpi = xi_ref[...].reshape(N, LANES).astype(BF16)

    wide = 4 * L
    for n0 in range(C * L // wide):
        y = (_dot(u2, m_ref[:, n0 * wide:(n0 + 1) * wide])
             + _dot(xpr, qr_ref[:, n0 * wide:(n0 + 1) * wide])
             + _dot(xpi, qi_ref[:, n0 * wide:(n0 + 1) * wide]))
        for cc in range(wide // L):
            co = n0 * (wide // L) + cc
            y_ref[co] = (y[:, cc * L:(cc + 1) * L] + d_ref[0, co] * u_ref[co].astype(F32)).astype(BF16)


def _ssm(zt3, row0, params, nbatch):
    (ar, ai, ldt, arc, aic, btr, bti, cr, ci, ctr, cti, d) = params
    G = ar.shape[0]
    C, L = SSM_GROUP, SSM_CHUNK
    N = zt3.shape[1]
    g3 = lambda g: (g, 0, 0)
    g4 = lambda g: (g, 0, 0, 0)
    return pl.pallas_call(
        functools.partial(_ssm_kernel, nbatch=nbatch),
        grid=(G,),
        in_specs=[
            pl.BlockSpec((C, N, L), lambda g: (row0 // C + g, 0, 0)),
            pl.BlockSpec((1, 1, LANES), g3), pl.BlockSpec((1, 1, LANES), g3), pl.BlockSpec((1, 1, 1), g3),
            pl.BlockSpec((1, LANES, 1), g3), pl.BlockSpec((1, LANES, 1), g3),
            pl.BlockSpec((1, C, LANES), g3), pl.BlockSpec((1, C, LANES), g3),
            pl.BlockSpec((1, C, LANES), g3), pl.BlockSpec((1, C, LANES), g3),
            pl.BlockSpec((1, C, LANES, 1), g4), pl.BlockSpec((1, C, LANES, 1), g4),
            pl.BlockSpec((1, C, 1, 1), g4),
        ],
        out_specs=pl.BlockSpec((C, N, L), g3),
        out_shape=jax.ShapeDtypeStruct((G * C, N, L), BF16),
        scratch_shapes=[
            pltpu.VMEM((N, C * L), BF16),
            pltpu.VMEM((C * L, C * L), BF16),
            pltpu.VMEM((C * L, LANES), BF16), pltpu.VMEM((C * L, LANES), BF16),
            pltpu.VMEM((LANES, C * L), BF16), pltpu.VMEM((LANES, C * L), BF16),
            pltpu.VMEM((C * C, L), F32),
            pltpu.VMEM((nbatch, N // nbatch, LANES), F32), pltpu.VMEM((nbatch, N // nbatch, LANES), F32),
            pltpu.VMEM((nbatch, N // nbatch, LANES), F32), pltpu.VMEM((nbatch, N // nbatch, LANES), F32),
        ],
        compiler_params=_cparams(("arbitrary",)),
        name="ssm",
    )(zt3, ar, ai, ldt, arc, aic, btr, bti, cr, ci, ctr, cti, d)


def _mixer_out_kernel(ga_ref, gb_ref, ys_ref, at_ref, x_ref, wglu_ref, wpa_ref, wout_ref, h_ref):
    dm = ga_ref.shape[0]
    nblk = at_ref.shape[1]
    att = jnp.concatenate(
        [jnp.concatenate([at_ref[h, j] for j in range(nblk)], axis=1) for h in range(ATT_HEADS)], axis=0)
    ya = _dot(wpa_ref[...], att)
    gl = jax.nn.gelu(ys_ref[...].astype(F32)).astype(BF16)
    zb = _dot(wglu_ref[...], gl)
    yb = zb[:dm] * jax.nn.sigmoid(zb[dm:])
    merged = jax.nn.sigmoid(ga_ref[...].astype(F32)) * ya + jax.nn.sigmoid(gb_ref[...].astype(F32)) * yb
    h_ref[...] = x_ref[...] + _dot_tn(merged.astype(BF16), wout_ref[...])


def _mixer_out(zt, ys_t, att4, x2, wglu_t, wpa_t, wout, tm=512):
    T, D = x2.shape
    nblk = tm // MOBA_BLOCK
    sw = ys_t.shape[0]
    full = lambda a: pl.BlockSpec(a.shape, lambda i: (0,) * a.ndim)
    return pl.pallas_call(
        _mixer_out_kernel,
        grid=(T // tm,),
        in_specs=[
            pl.BlockSpec((D, tm), lambda i: (0, i)),
            pl.BlockSpec((D, tm), lambda i: (1, i)),
            pl.BlockSpec((sw, tm), lambda i: (0, i)),
            pl.BlockSpec((ATT_HEADS, nblk, ATT_HEAD_DIM, MOBA_BLOCK), lambda i: (0, i, 0, 0)),
            pl.BlockSpec((tm, D), lambda i: (i, 0)),
            full(wglu_t), full(wpa_t), full(wout),
        ],
        out_specs=pl.BlockSpec((tm, D), lambda i: (i, 0)),
        out_shape=jax.ShapeDtypeStruct((T, D), F32),
        compiler_params=_cparams(("arbitrary",)),
        name="mixer_out",
    )(zt, zt, ys_t, att4, x2, wglu_t, wpa_t, wout)


def _memkv_kernel(mem_ref, g_ref, w_ref, kv_ref):
    kv_ref[...] = _dot(_rmsnorm(mem_ref[...], g_ref[...]).astype(BF16), w_ref[...]).astype(BF16)


def _memkv(mem2, g, w, rows):
    M, D = mem2.shape
    return pl.pallas_call(
        _memkv_kernel,
        grid=(M // rows,),
        in_specs=[pl.BlockSpec((rows, D), lambda i: (i, 0)), pl.BlockSpec((1, D), lambda i: (0, 0)),
                  pl.BlockSpec(w.shape, lambda i: (0, 0))],
        out_specs=pl.BlockSpec((rows, w.shape[1]), lambda i: (i, 0)),
        out_shape=jax.ShapeDtypeStruct((M, w.shape[1]), BF16),
        compiler_params=_cparams(("arbitrary",)),
        name="memkv",
    )(mem2, g, w)


def _xattn_kernel(h_ref, g_ref, wq_ref, kv_ref, wo_ref, o_ref):
    h = h_ref[...]
    q = _dot(_rmsnorm(h, g_ref[...]).astype(BF16), wq_ref[...]).astype(BF16)
    kv = kv_ref[...]
    xw = XATTN_HEADS * XATTN_HEAD_DIM
    scale = XATTN_HEAD_DIM ** -0.5
    outs = []
    for hd in range(XATTN_HEADS):
        sl = slice(hd * XATTN_HEAD_DIM, (hd + 1) * XATTN_HEAD_DIM)
        s = _dot_nt(q[:, sl], kv[:, sl]) * scale
        p = jnp.exp(s - jnp.max(s, axis=-1, keepdims=True))
        p = p / jnp.sum(p, axis=-1, keepdims=True)
        outs.append(_dot(p.astype(BF16), kv[:, xw + hd * XATTN_HEAD_DIM:xw + (hd + 1) * XATTN_HEAD_DIM]))
    o = jnp.concatenate(outs, axis=1).astype(BF16)
    o_ref[...] = h + _dot(o, wo_ref[...])


def _xattn(h1, g, wq, kv, wo, seq, mem_len, tm=512):
    T, D = h1.shape
    per_b = seq // tm
    return pl.pallas_call(
        _xattn_kernel,
        grid=(T // tm,),
        in_specs=[
            pl.BlockSpec((tm, D), lambda i: (i, 0)),
            pl.BlockSpec((1, D), lambda i: (0, 0)),
            pl.BlockSpec(wq.shape, lambda i: (0, 0)),
            pl.BlockSpec((mem_len, kv.shape[1]), lambda i: (i // per_b, 0)),
            pl.BlockSpec(wo.shape, lambda i: (0, 0)),
        ],
        out_specs=pl.BlockSpec((tm, D), lambda i: (i, 0)),
        out_shape=jax.ShapeDtypeStruct((T, D), F32),
        compiler_params=_cparams(("arbitrary",)),
        name="xattn",
    )(h1, g, wq, kv, wo)


def _router_kernel(h_ref, g_ref, wr_ref, br_ref, xn_ref, e_ref, r_ref, gate_ref, cnt_ref, carry_ref):
    i = pl.program_id(0)
    E = N_EXPERTS
    tm = h_ref.shape[0]

    @pl.when(i == 0)
    def _():
        carry_ref[...] = jnp.zeros_like(carry_ref)

    n = _rmsnorm(h_ref[...], g_ref[...])
    xn_ref[...] = n
    na, nb_, nc = _split3(n)
    wa, wb, wc = _split3(wr_ref[...])
    logits = (_dot_nt(wa, na) + (_dot_nt(wa, nb_) + _dot_nt(wb, na))
              + (_dot_nt(wb, nb_) + _dot_nt(wa, nc) + _dot_nt(wc, na))) + br_ref[...]
    eidx = lax.broadcasted_iota(I32, (E, tm), 0)
    g = logits
    picks, vals = [], []
    for _ in range(EXPERT_TOPK):
        mx = jnp.max(g, axis=0, keepdims=True)
        first = jnp.min(jnp.where(g == mx, eidx, E), axis=0, keepdims=True)
        pick = eidx == first
        picks.append(pick)
        vals.append(mx)
        g = jnp.where(pick, -3e38, g)
    ex = [jnp.exp(v - vals[0]) for v in vals]
    tot = ex[0] + ex[1] + ex[2] + ex[3]
    sel = jnp.zeros((E, tm), F32)
    for pk in picks:
        sel = jnp.where(pk, 1.0, sel)
    before = (lax.broadcasted_iota(I32, (tm, tm), 0) < lax.broadcasted_iota(I32, (tm, tm), 1))
    prefix = _dot(sel.astype(BF16), jnp.where(before, 1.0, 0.0).astype(BF16))
    pos = prefix + carry_ref[:, 0:1]
    zi = jnp.zeros((8 - EXPERT_TOPK, tm), I32)
    e_rows = [jnp.sum(jnp.where(pk, eidx, 0), axis=0, keepdims=True) for pk in picks]
    r_rows = [jnp.sum(jnp.where(pk, pos, 0.0), axis=0, keepdims=True).astype(I32) for pk in picks]
    e_ref[...] = jnp.concatenate(e_rows + [zi], axis=0)
    r_ref[...] = jnp.concatenate(r_rows + [zi], axis=0)
    gate_rows = jnp.concatenate([x / tot for x in ex] + [jnp.zeros((LANES - EXPERT_TOPK, tm), F32)], axis=0)
    gate_ref[...] = gate_rows.T
    carry_ref[...] = carry_ref[...] + jnp.sum(sel, axis=1, keepdims=True)
    cnt_ref[...] = carry_ref[...]


def _router(h2, g, wr_t, br, tm=256):
    T, D = h2.shape
    E = N_EXPERTS
    return pl.pallas_call(
        _router_kernel,
        grid=(T // tm,),
        in_specs=[pl.BlockSpec((tm, D), lambda i: (i, 0)), pl.BlockSpec((1, D), lambda i: (0, 0)),
                  pl.BlockSpec((E, D), lambda i: (0, 0)), pl.BlockSpec((E, 1), lambda i: (0, 0))],
        out_specs=[pl.BlockSpec((tm, D), lambda i: (i, 0)),
                   pl.BlockSpec((8, tm), lambda i: (0, i)), pl.BlockSpec((8, tm), lambda i: (0, i)),
                   pl.BlockSpec((tm, LANES), lambda i: (i, 0)),
                   pl.BlockSpec((E, LANES), lambda i: (0, 0))],
        out_shape=[jax.ShapeDtypeStruct((T, D), F32),
                   jax.ShapeDtypeStruct((8, T), I32), jax.ShapeDtypeStruct((8, T), I32),
                   jax.ShapeDtypeStruct((T, LANES), F32),
                   jax.ShapeDtypeStruct((E, LANES), F32)],
        scratch_shapes=[pltpu.VMEM((E, LANES), F32)],
        compiler_params=_cparams(("arbitrary",)),
        name="router",
    )(h2, g, wr_t, br)


def _row_copy(src, dst, sem):
    return pltpu.make_async_copy(src, dst, sem)


def _dispatch_kernel(ps_ref, zr_ref, e_ref, r_ref, xn_ref, xs_ref, zbuf, sem, zsem):
    tm = xn_ref.shape[0]

    @pl.when(pl.program_id(0) == 0)
    def _():
        zbuf[...] = jnp.zeros_like(zbuf)

        def zero_copy(e):
            return pltpu.make_async_copy(zbuf, xs_ref.at[pl.ds(pl.multiple_of(zr_ref[e], MOE_BLOCK), MOE_BLOCK)], zsem)

        for e in range(2 * N_EXPERTS):
            @pl.when(zr_ref[e] >= 0)
            def _():
                zero_copy(e).start()
        for e in range(2 * N_EXPERTS):
            @pl.when(zr_ref[e] >= 0)
            def _():
                zero_copy(e).wait()

    def copies(t):
        return [_row_copy(xn_ref.at[pl.ds(t, 1)], xs_ref.at[pl.ds(ps_ref[e_ref[k, t]] + r_ref[k, t], 1)], sem)
                for k in range(EXPERT_TOPK)]

    def start(t, c):
        for cp in copies(t):
            cp.start()
        return c

    def wait(t, c):
        for cp in copies(t):
            cp.wait()
        return c

    lax.fori_loop(0, tm, start, 0)
    lax.fori_loop(0, tm, wait, 0)


def _dispatch(pstart, zrow, e8, r8, xn, rows_total, tm=256):
    T, D = xn.shape
    grid_spec = pltpu.PrefetchScalarGridSpec(
        num_scalar_prefetch=2,
        grid=(T // tm,),
        in_specs=[pl.BlockSpec((8, tm), lambda i, ps, zr: (0, i), memory_space=pltpu.SMEM),
                  pl.BlockSpec((8, tm), lambda i, ps, zr: (0, i), memory_space=pltpu.SMEM),
                  pl.BlockSpec((tm, D), lambda i, ps, zr: (i, 0))],
        out_specs=pl.BlockSpec(memory_space=pl.ANY),
        scratch_shapes=[pltpu.VMEM((MOE_BLOCK, D), F32), pltpu.SemaphoreType.DMA(()), pltpu.SemaphoreType.DMA(())],
    )
    return pl.pallas_call(
        _dispatch_kernel,
        grid_spec=grid_spec,
        out_shape=jax.ShapeDtypeStruct((rows_total, D), F32),
        compiler_params=_cparams(("arbitrary",)),
        name="dispatch",
    )(pstart, zrow, e8, r8, xn)


def _experts_kernel(be_ref, nu_ref, xs_ref, wgu_ref, bgu_ref, wdn_ref, bdn_ref, ys_ref, wgu_bf, wdn_bf):
    i = pl.program_id(0)
    dff = wdn_ref.shape[1]

    @pl.when(i < nu_ref[0])
    def _():
        prev = be_ref[jnp.maximum(i - 1, 0)]

        @pl.when((i == 0) | (be_ref[i] != prev))
        def _():
            wgu_bf[...] = wgu_ref[0].astype(BF16)
            wdn_bf[...] = wdn_ref[0].astype(BF16)

        gu = _dot(xs_ref[...].astype(BF16), wgu_bf[...]) + bgu_ref[0]
        gate = jnp.minimum(gu[:, :dff], SWIGLU_LIMIT)
        up = jnp.clip(gu[:, dff:], -SWIGLU_LIMIT, SWIGLU_LIMIT)
        hid = (up + 1.0) * gate * jax.nn.sigmoid(SWIGLU_ALPHA * gate)
        ys_ref[...] = _dot(hid.astype(BF16), wdn_bf[...]) + bdn_ref[0]

    @pl.when(i >= nu_ref[0])
    def _():
        ys_ref[...] = jnp.zeros_like(ys_ref)


def _experts(blk_expert, nused, xs, wgu, bgu, wdn, bdn):
    rows, D = xs.shape
    E, _, F2 = wgu.shape
    dff = wdn.shape[1]
    nblk = rows // MOE_BLOCK
    grid_spec = pltpu.PrefetchScalarGridSpec(
        num_scalar_prefetch=2,
        grid=(nblk,),
        in_specs=[pl.BlockSpec((MOE_BLOCK, D), lambda i, be, nu: (jnp.minimum(i, nu[0] - 1), 0)),
                  pl.BlockSpec((1, D, F2), lambda i, be, nu: (be[i], 0, 0)),
                  pl.BlockSpec((1, 1, F2), lambda i, be, nu: (be[i], 0, 0)),
                  pl.BlockSpec((1, dff, D), lambda i, be, nu: (be[i], 0, 0)),
                  pl.BlockSpec((1, 1, D), lambda i, be, nu: (be[i], 0, 0))],
        out_specs=pl.BlockSpec((MOE_BLOCK, D), lambda i, be, nu: (i, 0)),
        scratch_shapes=[pltpu.VMEM((D, F2), BF16), pltpu.VMEM((dff, D), BF16)],
    )
    return pl.pallas_call(
        _experts_kernel,
        grid_spec=grid_spec,
        out_shape=jax.ShapeDtypeStruct((rows, D), F32),
        compiler_params=_cparams(("arbitrary",)),
        name="experts",
    )(blk_expert, nused, xs, wgu, bgu, wdn, bdn)


def _combine_kernel(ps_ref, e_ref, r_ref, h_ref, gate_ref, gf_ref, ys_ref, o_ref, buf, sem):
    tm = h_ref.shape[0]

    def copies(t):
        return [_row_copy(ys_ref.at[pl.ds(ps_ref[e_ref[k, t]] + r_ref[k, t], 1)], buf.at[k, pl.ds(t, 1)], sem)
                for k in range(EXPERT_TOPK)]

    def start(t, c):
        for cp in copies(t):
            cp.start()
        return c

    def wait(t, c):
        for cp in copies(t):
            cp.wait()
        return c

    lax.fori_loop(0, tm, start, 0)
    lax.fori_loop(0, tm, wait, 0)
    gates = gate_ref[...]
    acc = h_ref[...]
    for k in range(EXPERT_TOPK):
        acc = acc + gates[:, k:k + 1] * buf[k]
    o_ref[...] = _rmsnorm(acc, gf_ref[...])


def _combine(pstart, e8, r8, h2, gates, gf, ys, tm=256):
    T, D = h2.shape
    grid_spec = pltpu.PrefetchScalarGridSpec(
        num_scalar_prefetch=1,
        grid=(T // tm,),
        in_specs=[pl.BlockSpec((8, tm), lambda i, ps: (0, i), memory_space=pltpu.SMEM),
                  pl.BlockSpec((8, tm), lambda i, ps: (0, i), memory_space=pltpu.SMEM),
                  pl.BlockSpec((tm, D), lambda i, ps: (i, 0)),
                  pl.BlockSpec((tm, LANES), lambda i, ps: (i, 0)),
                  pl.BlockSpec((1, D), lambda i, ps: (0, 0)),
                  pl.BlockSpec(memory_space=pl.ANY)],
        out_specs=pl.BlockSpec((tm, D), lambda i, ps: (i, 0)),
        scratch_shapes=[pltpu.VMEM((EXPERT_TOPK, tm, D), F32), pltpu.SemaphoreType.DMA(())],
    )
    return pl.pallas_call(
        _combine_kernel,
        grid_spec=grid_spec,
        out_shape=jax.ShapeDtypeStruct((T, D), F32),
        compiler_params=_cparams(("arbitrary",)),
        name="combine",
    )(pstart, e8, r8, h2, gates, gf, ys)


def _pad_lanes(a, width=LANES, value=0.0):
    pad = [(0, 0)] * (a.ndim - 1) + [(0, width - a.shape[-1])]
    return jnp.pad(a, pad, constant_values=value)


def _layer(h2d, mem2d, batch, seq, mem_len, g_mix, w_in, a_re, a_im, log_dt, b_re, b_im, c_re, c_im, d_skip,
           w_glu, w_proj_a, w_out, g_xattn, g_mem, w_xq, w_xkv, w_xo, g_moe, w_router, b_router,
           w_gu, b_gu, w_dn, b_dn, g_final):
    T, D = h2d.shape
    aw = ATT_HEADS * ATT_HEAD_DIM
    G = a_re.shape[0]
    sw = G * SSM_GROUP

    wq, wk, wv, wu, wga, wgb = jnp.split(w_in, [aw, 2 * aw, 3 * aw, 3 * aw + sw, 3 * aw + sw + D], axis=1)
    wg_t = jnp.concatenate([wga, wgb, wu], axis=1).T.astype(BF16)
    wqv_t = jnp.concatenate([wq * (ATT_HEAD_DIM ** -0.5), wv], axis=1).T.astype(BF16)
    wk_pad = _pad_lanes(wk.reshape(D, ATT_HEADS, ATT_HEAD_DIM)).reshape(D, ATT_HEADS * LANES).astype(BF16)

    zt, q4, v4, kpad, km = _in_proj(h2d, g_mix.reshape(1, D), wg_t, wqv_t, wk_pad)
    nb = seq // MOBA_BLOCK
    att4 = _moba(q4, v4, kpad.reshape(T // MOBA_BLOCK, MOBA_BLOCK, ATT_HEADS * LANES),
                 km.reshape(batch, nb, ATT_HEADS * LANES), batch)

    row = lambda a, v=0.0: _pad_lanes(a, value=v).reshape(G, 1, LANES)
    col = lambda a, v=0.0: _pad_lanes(a, value=v).reshape(G, LANES, 1)
    ssm_params = (
        row(a_re, -1.0), row(a_im), log_dt.reshape(G, 1, 1), col(a_re, -1.0), col(a_im),
        _pad_lanes(jnp.swapaxes(b_re, 1, 2)), _pad_lanes(jnp.swapaxes(b_im, 1, 2)),
        _pad_lanes(c_re), _pad_lanes(c_im),
        _pad_lanes(c_re)[..., None], _pad_lanes(c_im)[..., None],
        d_skip.reshape(G, SSM_GROUP, 1, 1),
    )
    ys3 = _ssm(zt.reshape(zt.shape[0], T // SSM_CHUNK, SSM_CHUNK), 2 * D, ssm_params, batch)
    h1 = _mixer_out(zt, ys3.reshape(sw, T), att4, h2d, w_glu.T.astype(BF16), w_proj_a.T.astype(BF16),
                    w_out.astype(BF16))

    kv = _memkv(mem2d, g_mem.reshape(1, D), w_xkv.astype(BF16), mem_len)
    h2 = _xattn(h1, g_xattn.reshape(1, D), w_xq.astype(BF16), kv, w_xo.astype(BF16), seq, mem_len)

    xn, e8, r8, gates, cnt = _router(h2, g_moe.reshape(1, D), w_router.T, b_router.reshape(N_EXPERTS, 1))
    counts = cnt[:, 0].astype(I32)
    nblk_e = (counts + MOE_BLOCK - 1) // MOE_BLOCK
    bends = jnp.cumsum(nblk_e)
    pstart = ((bends - nblk_e) * MOE_BLOCK).astype(I32)
    nblk = (T * EXPERT_TOPK) // MOE_BLOCK + N_EXPERTS
    blk_expert = jnp.minimum(jnp.searchsorted(bends, jnp.arange(nblk, dtype=I32), side="right"),
                             N_EXPERTS - 1).astype(I32)
    nused = bends[-1:].astype(I32)
    blk_expert = jnp.where(jnp.arange(nblk) < nused[0], blk_expert, blk_expert[jnp.maximum(nused[0] - 1, 0)])

    tail = nblk - 1 - jnp.arange(N_EXPERTS, dtype=I32)
    zrow = jnp.concatenate([jnp.where(nblk_e > 0, (bends - 1) * MOE_BLOCK, -1),
                            jnp.where(tail >= nused[0], tail * MOE_BLOCK, -1)]).astype(I32)
    xs = _dispatch(pstart, zrow, e8, r8, xn, nblk * MOE_BLOCK)
    ys = _experts(blk_expert, nused, xs, w_gu, b_gu.reshape(N_EXPERTS, 1, -1), w_dn, b_dn.reshape(N_EXPERTS, 1, -1))
    return _combine(pstart, e8, r8, h2, gates, g_final.reshape(1, D), ys)


def kernel(x, mem, g_mix, w_in, a_re, a_im, log_dt, b_re, b_im, c_re, c_im, d_skip, w_glu, w_proj_a, w_out,
           g_xattn, g_mem, w_xq, w_xkv, w_xo, g_moe, w_router, b_router, w_gu, b_gu, w_dn, b_dn, g_final):
    B, S, D = x.shape
    M = mem.shape[1]
    assert g_mix.shape[0] == 1, "single layer"
    out = _layer(x.reshape(B * S, D), mem.reshape(B * M, D), B, S, M,
                 g_mix[0], w_in[0], a_re[0], a_im[0], log_dt[0], b_re[0], b_im[0], c_re[0], c_im[0], d_skip[0],
                 w_glu[0], w_proj_a[0], w_out[0], g_xattn[0], g_mem[0], w_xq[0], w_xkv[0], w_xo[0],
                 g_moe[0], w_router[0], b_router[0], w_gu[0], b_gu[0], w_dn[0], b_dn[0], g_final)
    return out.reshape(B, S, D)
```

```python
import functools
import math

import jax
import jax.numpy as jnp
from jax import lax
from jax.experimental import pallas as pl
from jax.experimental.pallas import tpu as pltpu

F32 = jnp.float32
BF16 = jnp.bfloat16
I32 = jnp.int32

RMS_EPS = 1e-6
NEG_INF = -1e30
ATT_HEADS = 8
ATT_HEAD_DIM = 64
MOBA_BLOCK = 256
MOBA_TOPK = 3
MOBA_HEADS_PER_STEP = 4
SSM_GROUP = 16
SSM_STATE = 64
SSM_CHUNK = 128
XATTN_HEADS = 4
XATTN_HEAD_DIM = 128
N_EXPERTS = 32
EXPERT_TOPK = 4
SWIGLU_LIMIT = 7.0
SWIGLU_ALPHA = 1.702
MOE_BLOCK = 256
LANES = 128
VMEM_LIMIT = 56 * 1024 * 1024


def _cparams(sem):
    return pltpu.CompilerParams(dimension_semantics=sem, vmem_limit_bytes=VMEM_LIMIT)


def _rmsnorm(x, g):
    return x * lax.rsqrt(jnp.mean(x * x, axis=-1, keepdims=True) + RMS_EPS) * g


def _dot(a, b):
    return jnp.dot(a, b, preferred_element_type=F32)


def _dot_nt(a, b):
    return lax.dot_general(a, b, (((1,), (1,)), ((), ())), preferred_element_type=F32)


def _dot_tn(a, b):
    return lax.dot_general(a, b, (((0,), (0,)), ((), ())), preferred_element_type=F32)


def _split3(x):
    a = x.astype(BF16)
    r = x - a.astype(F32)
    b = r.astype(BF16)
    c = (r - b.astype(F32)).astype(BF16)
    return a, b, c


def _inproj_kernel(x_ref, g_ref, wg_ref, wqv_ref, wk_ref, zt_ref, q_ref, v_ref, k_ref, km_ref):
    n = _rmsnorm(x_ref[...], g_ref[...]).astype(BF16)
    tm = n.shape[0]
    nblk = tm // MOBA_BLOCK
    rows = 512
    for c in range(wg_ref.shape[0] // rows):
        zt_ref[c * rows:(c + 1) * rows, :] = _dot_nt(wg_ref[c * rows:(c + 1) * rows, :], n).astype(BF16)
    for c, dst in ((0, q_ref), (1, v_ref)):
        r = _dot_nt(wqv_ref[c * rows:(c + 1) * rows, :], n).astype(BF16)
        for h in range(ATT_HEADS):
            for j in range(nblk):
                dst[h, j] = r[h * ATT_HEAD_DIM:(h + 1) * ATT_HEAD_DIM, j * MOBA_BLOCK:(j + 1) * MOBA_BLOCK]
    k = _dot(n, wk_ref[...])
    lane = lax.broadcasted_iota(I32, (1, k.shape[1]), 1) & (LANES - 1)
    k = k + jnp.where(lane == ATT_HEAD_DIM, 1.0, 0.0)
    k_ref[...] = k.astype(BF16)
    km_ref[0] = jnp.mean(k.reshape(nblk, MOBA_BLOCK, k.shape[1]), axis=1)


def _in_proj(x2, g, wg_t, wqv_t, wk_pad, tm=512):
    T, D = x2.shape
    nblk = tm // MOBA_BLOCK
    ng = wg_t.shape[0]
    kw = wk_pad.shape[1]
    hd = (ATT_HEADS, T // MOBA_BLOCK, ATT_HEAD_DIM, MOBA_BLOCK)
    return pl.pallas_call(
        _inproj_kernel,
        grid=(T // tm,),
        in_specs=[
            pl.BlockSpec((tm, D), lambda i: (i, 0)),
            pl.BlockSpec((1, D), lambda i: (0, 0)),
            pl.BlockSpec((ng, D), lambda i: (0, 0)),
            pl.BlockSpec(wqv_t.shape, lambda i: (0, 0)),
            pl.BlockSpec((D, kw), lambda i: (0, 0)),
        ],
        out_specs=[
            pl.BlockSpec((ng, tm), lambda i: (0, i)),
            pl.BlockSpec((ATT_HEADS, nblk, ATT_HEAD_DIM, MOBA_BLOCK), lambda i: (0, i, 0, 0)),
            pl.BlockSpec((ATT_HEADS, nblk, ATT_HEAD_DIM, MOBA_BLOCK), lambda i: (0, i, 0, 0)),
            pl.BlockSpec((tm, kw), lambda i: (i, 0)),
            pl.BlockSpec((1, nblk, kw), lambda i: (i, 0, 0)),
        ],
        out_shape=[
            jax.ShapeDtypeStruct((ng, T), BF16),
            jax.ShapeDtypeStruct(hd, BF16),
            jax.ShapeDtypeStruct(hd, BF16),
            jax.ShapeDtypeStruct((T, kw), BF16),
            jax.ShapeDtypeStruct((T // tm, nblk, kw), F32),
        ],
        compiler_params=_cparams(("arbitrary",)),
        name="in_proj",
    )(x2, g, wg_t, wqv_t, wk_pad)


def _moba_kernel(q_ref, v_ref, k_ref, km_ref, o_ref, sel_ref, acc_ref, sa_ref, sb_ref, pb_ref, *, nb, hb):
    blk = MOBA_BLOCK
    heads = range(hb)
    lanes = lambda h: slice(h * LANES, (h + 1) * LANES)
    kms = [_split3(km_ref[0][:, lanes(h)]) for h in heads]
    zpad = jnp.zeros((LANES - ATT_HEAD_DIM, blk), BF16)
    bidx = lax.broadcasted_iota(I32, (nb, blk), 0)
    causal = lax.broadcasted_iota(I32, (blk, blk), 0) <= lax.broadcasted_iota(I32, (blk, blk), 1)
    topk = min(MOBA_TOPK, nb)

    row16 = lax.broadcasted_iota(I32, (16, blk), 0) == 0
    ones16 = jnp.where(row16, 1.0, 0.0).astype(BF16)
    zpad48 = jnp.zeros((LANES - ATT_HEAD_DIM - 16, blk), BF16)

    def qblock(i, carry):
        elig = bidx < i
        q64 = [q_ref[h, i] for h in heads]
        for h in heads:
            q = jnp.concatenate([q64[h], zpad], axis=0)
            gate = (_dot(kms[h][0], q) + _dot(kms[h][1], q)) + _dot(kms[h][2], q)
            g = jnp.where(elig, gate, NEG_INF)
            sel = jnp.zeros((nb, blk), F32)
            for _ in range(topk):
                mx = jnp.max(g, axis=0, keepdims=True)
                first = jnp.min(jnp.where(g == mx, bidx, nb), axis=0, keepdims=True)
                pick = bidx == first
                sel = jnp.where(pick, 1.0, sel)
                g = jnp.where(pick, -3e38, g)
            sel_ref[h] = jnp.where(elig, sel, 0.0)
            acc_ref[h] = jnp.zeros((ATT_HEAD_DIM + 16, blk), F32)
            pb_ref[h] = jnp.zeros((blk, blk), BF16)

        def scores(h, t):
            bias = (sel_ref[h, pl.ds(t, 1), :] - 1.0) * 1e30
            q = jnp.concatenate([q64[h], jnp.where(row16, bias, 0.0).astype(BF16), zpad48], axis=0)
            return _dot(k_ref[t][:, lanes(h)], q)

        def values(h, t, p):
            return _dot(jnp.concatenate([v_ref[h, t], ones16], axis=0), p)

        def softmax(s, m):
            m_new = jnp.maximum(m, jnp.max(s, axis=0, keepdims=True))
            return m_new, jnp.exp(m - m_new), jnp.exp(s - m_new).astype(BF16)

        for h in heads:
            sa_ref[h] = scores(h, 0)

        def kvpair(u, c):
            ta, tb, tn = 2 * u, 2 * u + 1, jnp.minimum(2 * u + 2, nb - 1)
            pv = [values(h, jnp.maximum(ta - 1, 0), pb_ref[h]) for h in heads]
            sb = [scores(h, tb) for h in heads]
            for h in heads:
                sb_ref[h] = sb[h]
                acc_ref[h] = acc_ref[h] * c[2 * h + 1] + pv[h]
            sm = [softmax(sa_ref[h], c[2 * h]) for h in heads]
            pv = [values(h, ta, sm[h][2]) for h in heads]
            sa = [scores(h, tn) for h in heads]
            out = []
            for h in heads:
                sa_ref[h] = sa[h]
                acc_ref[h] = acc_ref[h] * sm[h][1] + pv[h]
                m_new, alpha, p = softmax(sb_ref[h], sm[h][0])
                pb_ref[h] = p
                out += [m_new, alpha]
            return tuple(out)

        m0 = jnp.full((1, blk), -1e29, F32)
        npair = (i + 1) // 2
        c = lax.fori_loop(0, npair, kvpair, (m0, jnp.ones((1, blk), F32)) * hb)
        pv = [values(h, jnp.maximum(2 * npair - 1, 0), pb_ref[h]) for h in heads]
        kd = k_ref[i]
        own = [_dot(kd[:, lanes(h)], jnp.concatenate([q64[h], zpad], axis=0)) for h in heads]
        sm = [softmax(jnp.where(causal, own[h], NEG_INF), c[2 * h]) for h in heads]
        for h in heads:
            res = (acc_ref[h] * c[2 * h + 1] + pv[h]) * sm[h][1] + values(h, i, sm[h][2])
            o_ref[h, i] = (res[:ATT_HEAD_DIM] / res[ATT_HEAD_DIM:ATT_HEAD_DIM + 1]).astype(BF16)
        return carry

    lax.fori_loop(0, nb, qblock, 0)


def _moba(q4, v4, k3, km3, batch, hb=MOBA_HEADS_PER_STEP):
    H, nblk_total, dh, blk = q4.shape
    nb = nblk_total // batch
    return pl.pallas_call(
        functools.partial(_moba_kernel, nb=nb, hb=hb),
        grid=(batch, H // hb),
        in_specs=[
            pl.BlockSpec((hb, nb, dh, blk), lambda b, h: (h, b, 0, 0)),
            pl.BlockSpec((hb, nb, dh, blk), lambda b, h: (h, b, 0, 0)),
            pl.BlockSpec((nb, blk, hb * LANES), lambda b, h: (b, 0, h)),
            pl.BlockSpec((1, nb, hb * LANES), lambda b, h: (b, 0, h)),
        ],
        out_specs=pl.BlockSpec((hb, nb, dh, blk), lambda b, h: (h, b, 0, 0)),
        out_shape=jax.ShapeDtypeStruct(q4.shape, BF16),
        scratch_shapes=[pltpu.VMEM((hb, nb, blk), F32),
                        pltpu.VMEM((hb, dh + 16, blk), F32),
                        pltpu.VMEM((hb, blk, blk), F32),
                        pltpu.VMEM((hb, blk, blk), F32),
                        pltpu.VMEM((hb, blk, blk), BF16)],
        compiler_params=_cparams(("arbitrary", "arbitrary")),
        name="moba",
    )(q4, v4, k3, km3)


def _ssm_kernel(u_ref, ar_ref, ai_ref, ldt_ref, arc_ref, aic_ref, btr_ref, bti_ref, cr_ref, ci_ref,
                ctr_ref, cti_ref, d_ref, y_ref,
                u2_ref, m_ref, pr_ref, pi_ref, qr_ref, qi_ref, kv_ref, sr_ref, si_ref, xr_ref, xi_ref,
                *, nbatch):
    L = SSM_CHUNK
    C = SSM_GROUP
    N = u_ref.shape[1]
    nchunk = N // nbatch
    hi = lax.Precision.HIGHEST

    dt = jnp.exp(ldt_ref[0])
    are = jnp.minimum(ar_ref[0], -1e-4)
    aim = ai_ref[0]
    ea, th = are * dt, aim * dt
    mag = jnp.exp(ea)
    lbr, lbi = mag * jnp.cos(th), mag * jnp.sin(th)
    den = are * are + aim * aim
    cfr = ((lbr - 1.0) * are + lbi * aim) / den
    cfi = (lbi * are - (lbr - 1.0) * aim) / den
    btr, bti = btr_ref[0], bti_ref[0]
    bbr = cfr * btr - cfi * bti
    bbi = cfr * bti + cfi * btr
    crr, cri = cr_ref[0], ci_ref[0]

    eac = jnp.minimum(arc_ref[0], -1e-4) * dt
    thc = aic_ref[0] * dt
    tau = lax.broadcasted_iota(I32, (1, L), 1).astype(F32)
    m0 = jnp.exp(eac * tau)
    pw0r, pw0i = m0 * jnp.cos(thc * tau), m0 * jnp.sin(thc * tau)
    m1 = jnp.exp(eac * (tau + 1.0))
    pw1r, pw1i = m1 * jnp.cos(thc * (tau + 1.0)), m1 * jnp.sin(thc * (tau + 1.0))
    back = (L - 1.0) - lax.broadcasted_iota(I32, (L, 1), 0).astype(F32)
    mb = jnp.exp(ea * back)
    pbr, pbi = mb * jnp.cos(th * back), mb * jnp.sin(th * back)
    ml = jnp.exp(ea * float(L))
    alr, ali = ml * jnp.cos(th * float(L)), ml * jnp.sin(th * float(L))

    upper = lax.broadcasted_iota(I32, (L, L), 1) >= lax.broadcasted_iota(I32, (L, L), 0)
    for ci in range(C):
        cbr = bbr[ci:ci + 1, :] * crr - bbi[ci:ci + 1, :] * cri
        cbi = bbr[ci:ci + 1, :] * cri + bbi[ci:ci + 1, :] * crr
        kv_ref[ci * C:(ci + 1) * C, :] = (jnp.dot(cbr, pw0r, precision=hi, preferred_element_type=F32)
                                          - jnp.dot(cbi, pw0i, precision=hi, preferred_element_type=F32))
        pr_ref[ci * L:(ci + 1) * L, :] = (pbr * bbr[ci:ci + 1, :] - pbi * bbi[ci:ci + 1, :]).astype(BF16)
        pi_ref[ci * L:(ci + 1) * L, :] = (pbr * bbi[ci:ci + 1, :] + pbi * bbr[ci:ci + 1, :]).astype(BF16)
        u2_ref[:, ci * L:(ci + 1) * L] = u_ref[ci]
    for co in range(C):
        cc_r, cc_i = ctr_ref[0, co], cti_ref[0, co]
        qr_ref[:, co * L:(co + 1) * L] = (cc_r * pw1r - cc_i * pw1i).astype(BF16)
        qi_ref[:, co * L:(co + 1) * L] = (-(cc_r * pw1i + cc_i * pw1r)).astype(BF16)

    def toeplitz(ci, carry):
        for co in range(C):
            taps = kv_ref[pl.ds(ci * C + co, 1), :]
            t = pltpu.roll(jnp.broadcast_to(taps, (L, L)), 0, 1, stride=1, stride_axis=0)
            m_ref[pl.ds(pl.multiple_of(ci * L, L), L), co * L:(co + 1) * L] = jnp.where(upper, t, 0.0).astype(BF16)
        return carry

    lax.fori_loop(0, C, toeplitz, 0)

    u2 = u2_ref[...]
    sr_ref[...] = _dot(u2, pr_ref[...]).reshape(nbatch, nchunk, LANES)
    si_ref[...] = _dot(u2, pi_ref[...]).reshape(nbatch, nchunk, LANES)

    def chunk_scan(c, carry):
        xr, xi = carry
        xr_ref[:, pl.ds(c, 1), :] = xr
        xi_ref[:, pl.ds(c, 1), :] = xi
        nr = alr * xr - ali * xi + sr_ref[:, pl.ds(c, 1), :]
        ni = alr * xi + ali * xr + si_ref[:, pl.ds(c, 1), :]
        return nr, ni

    zero = jnp.zeros((nbatch, 1, LANES), F32)
    lax.fori_loop(0, nchunk, chunk_scan, (zero, zero))
    xpr = xr_ref[...].reshape(N, LANES).astype(BF16)
    xpi = xi_ref[...].reshape(N, LANES).astype(BF16)

    wide = 4 * L
    for n0 in range(C * L // wide):
        y = (_dot(u2, m_ref[:, n0 * wide:(n0 + 1) * wide])
             + _dot(xpr, qr_ref[:, n0 * wide:(n0 + 1) * wide])
             + _dot(xpi, qi_ref[:, n0 * wide:(n0 + 1) * wide]))
        for cc in range(wide // L):
            co = n0 * (wide // L) + cc
            y_ref[co] = (y[:, cc * L:(cc + 1) * L] + d_ref[0, co] * u_ref[co].astype(F32)).astype(BF16)


def _ssm(zt3, row0, params, nbatch):
    (ar, ai, ldt, arc, aic, btr, bti, cr, ci, ctr, cti, d) = params
    G = ar.shape[0]
    C, L = SSM_GROUP, SSM_CHUNK
    N = zt3.shape[1]
    g3 = lambda g: (g, 0, 0)
    g4 = lambda g: (g, 0, 0, 0)
    return pl.pallas_call(
        functools.partial(_ssm_kernel, nbatch=nbatch),
        grid=(G,),
        in_specs=[
            pl.BlockSpec((C, N, L), lambda g: (row0 // C + g, 0, 0)),
            pl.BlockSpec((1, 1, LANES), g3), pl.BlockSpec((1, 1, LANES), g3), pl.BlockSpec((1, 1, 1), g3),
            pl.BlockSpec((1, LANES, 1), g3), pl.BlockSpec((1, LANES, 1), g3),
            pl.BlockSpec((1, C, LANES), g3), pl.BlockSpec((1, C, LANES), g3),
            pl.BlockSpec((1, C, LANES), g3), pl.BlockSpec((1, C, LANES), g3),
            pl.BlockSpec((1, C, LANES, 1), g4), pl.BlockSpec((1, C, LANES, 1), g4),
            pl.BlockSpec((1, C, 1, 1), g4),
        ],
        out_specs=pl.BlockSpec((C, N, L), g3),
        out_shape=jax.ShapeDtypeStruct((G * C, N, L), BF16),
        scratch_shapes=[
            pltpu.VMEM((N, C * L), BF16),
            pltpu.VMEM((C * L, C * L), BF16),
            pltpu.VMEM((C * L, LANES), BF16), pltpu.VMEM((C * L, LANES), BF16),
            pltpu.VMEM((LANES, C * L), BF16), pltpu.VMEM((LANES, C * L), BF16),
            pltpu.VMEM((C * C, L), F32),
            pltpu.VMEM((nbatch, N // nbatch, LANES), F32), pltpu.VMEM((nbatch, N // nbatch, LANES), F32),
            pltpu.VMEM((nbatch, N // nbatch, LANES), F32), pltpu.VMEM((nbatch, N // nbatch, LANES), F32),
        ],
        compiler_params=_cparams(("arbitrary",)),
        name="ssm",
    )(zt3, ar, ai, ldt, arc, aic, btr, bti, cr, ci, ctr, cti, d)


def _mixer_out_kernel(ga_ref, gb_ref, ys_ref, at_ref, x_ref, wglu_ref, wpa_ref, wout_ref, h_ref):
    dm = ga_ref.shape[0]
    nblk = at_ref.shape[1]
    att = jnp.concatenate(
        [jnp.concatenate([at_ref[h, j] for j in range(nblk)], axis=1) for h in range(ATT_HEADS)], axis=0)
    ya = _dot(wpa_ref[...], att)
    gl = jax.nn.gelu(ys_ref[...].astype(F32)).astype(BF16)
    zb = _dot(wglu_ref[...], gl)
    yb = zb[:dm] * jax.nn.sigmoid(zb[dm:])
    merged = jax.nn.sigmoid(ga_ref[...].astype(F32)) * ya + jax.nn.sigmoid(gb_ref[...].astype(F32)) * yb
    h_ref[...] = x_ref[...] + _dot_tn(merged.astype(BF16), wout_ref[...])


def _mixer_out(zt, ys_t, att4, x2, wglu_t, wpa_t, wout, tm=512):
    T, D = x2.shape
    nblk = tm // MOBA_BLOCK
    sw = ys_t.shape[0]
    full = lambda a: pl.BlockSpec(a.shape, lambda i: (0,) * a.ndim)
    return pl.pallas_call(
        _mixer_out_kernel,
        grid=(T // tm,),
        in_specs=[
            pl.BlockSpec((D, tm), lambda i: (0, i)),
            pl.BlockSpec((D, tm), lambda i: (1, i)),
            pl.BlockSpec((sw, tm), lambda i: (0, i)),
            pl.BlockSpec((ATT_HEADS, nblk, ATT_HEAD_DIM, MOBA_BLOCK), lambda i: (0, i, 0, 0)),
            pl.BlockSpec((tm, D), lambda i: (i, 0)),
            full(wglu_t), full(wpa_t), full(wout),
        ],
        out_specs=pl.BlockSpec((tm, D), lambda i: (i, 0)),
        out_shape=jax.ShapeDtypeStruct((T, D), F32),
        compiler_params=_cparams(("arbitrary",)),
        name="mixer_out",
    )(zt, zt, ys_t, att4, x2, wglu_t, wpa_t, wout)


def _memkv_kernel(mem_ref, g_ref, w_ref, kv_ref):
    kv_ref[...] = _dot(_rmsnorm(mem_ref[...], g_ref[...]).astype(BF16), w_ref[...]).astype(BF16)


def _memkv(mem2, g, w, rows):
    M, D = mem2.shape
    return pl.pallas_call(
        _memkv_kernel,
        grid=(M // rows,),
        in_specs=[pl.BlockSpec((rows, D), lambda i: (i, 0)), pl.BlockSpec((1, D), lambda i: (0, 0)),
                  pl.BlockSpec(w.shape, lambda i: (0, 0))],
        out_specs=pl.BlockSpec((rows, w.shape[1]), lambda i: (i, 0)),
        out_shape=jax.ShapeDtypeStruct((M, w.shape[1]), BF16),
        compiler_params=_cparams(("arbitrary",)),
        name="memkv",
    )(mem2, g, w)


def _xattn_kernel(h_ref, g_ref, wq_ref, kv_ref, wo_ref, o_ref):
    h = h_ref[...]
    q = _dot(_rmsnorm(h, g_ref[...]).astype(BF16), wq_ref[...]).astype(BF16)
    kv = kv_ref[...]
    xw = XATTN_HEADS * XATTN_HEAD_DIM
    scale = XATTN_HEAD_DIM ** -0.5
    outs = []
    for hd in range(XATTN_HEADS):
        sl = slice(hd * XATTN_HEAD_DIM, (hd + 1) * XATTN_HEAD_DIM)
        s = _dot_nt(q[:, sl], kv[:, sl]) * scale
        p = jnp.exp(s - jnp.max(s, axis=-1, keepdims=True))
        p = p / jnp.sum(p, axis=-1, keepdims=True)
        outs.append(_dot(p.astype(BF16), kv[:, xw + hd * XATTN_HEAD_DIM:xw + (hd + 1) * XATTN_HEAD_DIM]))
    o = jnp.concatenate(outs, axis=1).astype(BF16)
    o_ref[...] = h + _dot(o, wo_ref[...])


def _xattn(h1, g, wq, kv, wo, seq, mem_len, tm=512):
    T, D = h1.shape
    per_b = seq // tm
    return pl.pallas_call(
        _xattn_kernel,
        grid=(T // tm,),
        in_specs=[
            pl.BlockSpec((tm, D), lambda i: (i, 0)),
            pl.BlockSpec((1, D), lambda i: (0, 0)),
            pl.BlockSpec(wq.shape, lambda i: (0, 0)),
            pl.BlockSpec((mem_len, kv.shape[1]), lambda i: (i // per_b, 0)),
            pl.BlockSpec(wo.shape, lambda i: (0, 0)),
        ],
        out_specs=pl.BlockSpec((tm, D), lambda i: (i, 0)),
        out_shape=jax.ShapeDtypeStruct((T, D), F32),
        compiler_params=_cparams(("arbitrary",)),
        name="xattn",
    )(h1, g, wq, kv, wo)


def _router_kernel(h_ref, g_ref, wr_ref, br_ref, xn_ref, e_ref, r_ref, gate_ref, cnt_ref, carry_ref):
    i = pl.program_id(0)
    E = N_EXPERTS
    tm = h_ref.shape[0]

    @pl.when(i == 0)
    def _():
        carry_ref[...] = jnp.zeros_like(carry_ref)

    n = _rmsnorm(h_ref[...], g_ref[...])
    xn_ref[...] = n
    na, nb_, nc = _split3(n)
    wa, wb, wc = _split3(wr_ref[...])
    logits = (_dot_nt(wa, na) + (_dot_nt(wa, nb_) + _dot_nt(wb, na))
              + (_dot_nt(wb, nb_) + _dot_nt(wa, nc) + _dot_nt(wc, na))) + br_ref[...]
    eidx = lax.broadcasted_iota(I32, (E, tm), 0)
    g = logits
    picks, vals = [], []
    for _ in range(EXPERT_TOPK):
        mx = jnp.max(g, axis=0, keepdims=True)
        first = jnp.min(jnp.where(g == mx, eidx, E), axis=0, keepdims=True)
        pick = eidx == first
        picks.append(pick)
        vals.append(mx)
        g = jnp.where(pick, -3e38, g)
    ex = [jnp.exp(v - vals[0]) for v in vals]
    tot = ex[0] + ex[1] + ex[2] + ex[3]
    sel = jnp.zeros((E, tm), F32)
    for pk in picks:
        sel = jnp.where(pk, 1.0, sel)
    before = (lax.broadcasted_iota(I32, (tm, tm), 0) < lax.broadcasted_iota(I32, (tm, tm), 1))
    prefix = _dot(sel.astype(BF16), jnp.where(before, 1.0, 0.0).astype(BF16))
    pos = prefix + carry_ref[:, 0:1]
    zi = jnp.zeros((8 - EXPERT_TOPK, tm), I32)
    e_rows = [jnp.sum(jnp.where(pk, eidx, 0), axis=0, keepdims=True) for pk in picks]
    r_rows = [jnp.sum(jnp.where(pk, pos, 0.0), axis=0, keepdims=True).astype(I32) for pk in picks]
    e_ref[...] = jnp.concatenate(e_rows + [zi], axis=0)
    r_ref[...] = jnp.concatenate(r_rows + [zi], axis=0)
    gate_rows = jnp.concatenate([x / tot for x in ex] + [jnp.zeros((LANES - EXPERT_TOPK, tm), F32)], axis=0)
    gate_ref[...] = gate_rows.T
    carry_ref[...] = carry_ref[...] + jnp.sum(sel, axis=1, keepdims=True)
    cnt_ref[...] = carry_ref[...]


def _router(h2, g, wr_t, br, tm=256):
    T, D = h2.shape
    E = N_EXPERTS
    return pl.pallas_call(
        _router_kernel,
        grid=(T // tm,),
        in_specs=[pl.BlockSpec((tm, D), lambda i: (i, 0)), pl.BlockSpec((1, D), lambda i: (0, 0)),
                  pl.BlockSpec((E, D), lambda i: (0, 0)), pl.BlockSpec((E, 1), lambda i: (0, 0))],
        out_specs=[pl.BlockSpec((tm, D), lambda i: (i, 0)),
                   pl.BlockSpec((8, tm), lambda i: (0, i)), pl.BlockSpec((8, tm), lambda i: (0, i)),
                   pl.BlockSpec((tm, LANES), lambda i: (i, 0)),
                   pl.BlockSpec((E, LANES), lambda i: (0, 0))],
        out_shape=[jax.ShapeDtypeStruct((T, D), F32),
                   jax.ShapeDtypeStruct((8, T), I32), jax.ShapeDtypeStruct((8, T), I32),
                   jax.ShapeDtypeStruct((T, LANES), F32),
                   jax.ShapeDtypeStruct((E, LANES), F32)],
        scratch_shapes=[pltpu.VMEM((E, LANES), F32)],
        compiler_params=_cparams(("arbitrary",)),
        name="router",
    )(h2, g, wr_t, br)


def _row_copy(src, dst, sem):
    return pltpu.make_async_copy(src, dst, sem)


def _dispatch_kernel(ps_ref, zr_ref, e_ref, r_ref, xn_ref, xs_ref, zbuf, sem, zsem):
    tm = xn_ref.shape[0]

    @pl.when(pl.program_id(0) == 0)
    def _():
        zbuf[...] = jnp.zeros_like(zbuf)

        def zero_copy(e):
            return pltpu.make_async_copy(zbuf, xs_ref.at[pl.ds(pl.multiple_of(zr_ref[e], MOE_BLOCK), MOE_BLOCK)], zsem)

        for e in range(2 * N_EXPERTS):
            @pl.when(zr_ref[e] >= 0)
            def _():
                zero_copy(e).start()
        for e in range(2 * N_EXPERTS):
            @pl.when(zr_ref[e] >= 0)
            def _():
                zero_copy(e).wait()

    def copies(t):
        return [_row_copy(xn_ref.at[pl.ds(t, 1)], xs_ref.at[pl.ds(ps_ref[e_ref[k, t]] + r_ref[k, t], 1)], sem)
                for k in range(EXPERT_TOPK)]

    def start(t, c):
        for cp in copies(t):
            cp.start()
        return c

    def wait(t, c):
        for cp in copies(t):
            cp.wait()
        return c

    lax.fori_loop(0, tm, start, 0)
    lax.fori_loop(0, tm, wait, 0)


def _dispatch(pstart, zrow, e8, r8, xn, rows_total, tm=256):
    T, D = xn.shape
    grid_spec = pltpu.PrefetchScalarGridSpec(
        num_scalar_prefetch=2,
        grid=(T // tm,),
        in_specs=[pl.BlockSpec((8, tm), lambda i, ps, zr: (0, i), memory_space=pltpu.SMEM),
                  pl.BlockSpec((8, tm), lambda i, ps, zr: (0, i), memory_space=pltpu.SMEM),
                  pl.BlockSpec((tm, D), lambda i, ps, zr: (i, 0))],
        out_specs=pl.BlockSpec(memory_space=pl.ANY),
        scratch_shapes=[pltpu.VMEM((MOE_BLOCK, D), F32), pltpu.SemaphoreType.DMA(()), pltpu.SemaphoreType.DMA(())],
    )
    return pl.pallas_call(
        _dispatch_kernel,
        grid_spec=grid_spec,
        out_shape=jax.ShapeDtypeStruct((rows_total, D), F32),
        compiler_params=_cparams(("arbitrary",)),
        name="dispatch",
    )(pstart, zrow, e8, r8, xn)


def _experts_kernel(be_ref, nu_ref, xs_ref, wgu_ref, bgu_ref, wdn_ref, bdn_ref, ys_ref, wgu_bf, wdn_bf):
    i = pl.program_id(0)
    dff = wdn_ref.shape[1]

    @pl.when(i < nu_ref[0])
    def _():
        prev = be_ref[jnp.maximum(i - 1, 0)]

        @pl.when((i == 0) | (be_ref[i] != prev))
        def _():
            wgu_bf[...] = wgu_ref[0].astype(BF16)
            wdn_bf[...] = wdn_ref[0].astype(BF16)

        gu = _dot(xs_ref[...].astype(BF16), wgu_bf[...]) + bgu_ref[0]
        gate = jnp.minimum(gu[:, :dff], SWIGLU_LIMIT)
        up = jnp.clip(gu[:, dff:], -SWIGLU_LIMIT, SWIGLU_LIMIT)
        hid = (up + 1.0) * gate * jax.nn.sigmoid(SWIGLU_ALPHA * gate)
        ys_ref[...] = _dot(hid.astype(BF16), wdn_bf[...]) + bdn_ref[0]

    @pl.when(i >= nu_ref[0])
    def _():
        ys_ref[...] = jnp.zeros_like(ys_ref)


def _experts(blk_expert, nused, xs, wgu, bgu, wdn, bdn):
    rows, D = xs.shape
    E, _, F2 = wgu.shape
    dff = wdn.shape[1]
    nblk = rows // MOE_BLOCK
    grid_spec = pltpu.PrefetchScalarGridSpec(
        num_scalar_prefetch=2,
        grid=(nblk,),
        in_specs=[pl.BlockSpec((MOE_BLOCK, D), lambda i, be, nu: (jnp.minimum(i, nu[0] - 1), 0)),
                  pl.BlockSpec((1, D, F2), lambda i, be, nu: (be[i], 0, 0)),
                  pl.BlockSpec((1, 1, F2), lambda i, be, nu: (be[i], 0, 0)),
                  pl.BlockSpec((1, dff, D), lambda i, be, nu: (be[i], 0, 0)),
                  pl.BlockSpec((1, 1, D), lambda i, be, nu: (be[i], 0, 0))],
        out_specs=pl.BlockSpec((MOE_BLOCK, D), lambda i, be, nu: (i, 0)),
        scratch_shapes=[pltpu.VMEM((D, F2), BF16), pltpu.VMEM((dff, D), BF16)],
    )
    return pl.pallas_call(
        _experts_kernel,
        grid_spec=grid_spec,
        out_shape=jax.ShapeDtypeStruct((rows, D), F32),
        compiler_params=_cparams(("arbitrary",)),
        name="experts",
    )(blk_expert, nused, xs, wgu, bgu, wdn, bdn)


def _combine_kernel(ps_ref, e_ref, r_ref, h_ref, gate_ref, gf_ref, ys_ref, o_ref, buf, sem):
    tm = h_ref.shape[0]

    def copies(t):
        return [_row_copy(ys_ref.at[pl.ds(ps_ref[e_ref[k, t]] + r_ref[k, t], 1)], buf.at[k, pl.ds(t, 1)], sem)
                for k in range(EXPERT_TOPK)]

    def start(t, c):
        for cp in copies(t):
            cp.start()
        return c

    def wait(t, c):
        for cp in copies(t):
            cp.wait()
        return c

    lax.fori_loop(0, tm, start, 0)
    lax.fori_loop(0, tm, wait, 0)
    gates = gate_ref[...]
    acc = h_ref[...]
    for k in range(EXPERT_TOPK):
        acc = acc + gates[:, k:k + 1] * buf[k]
    o_ref[...] = _rmsnorm(acc, gf_ref[...])


def _combine(pstart, e8, r8, h2, gates, gf, ys, tm=256):
    T, D = h2.shape
    grid_spec = pltpu.PrefetchScalarGridSpec(
        num_scalar_prefetch=1,
        grid=(T // tm,),
        in_specs=[pl.BlockSpec((8, tm), lambda i, ps: (0, i), memory_space=pltpu.SMEM),
                  pl.BlockSpec((8, tm), lambda i, ps: (0, i), memory_space=pltpu.SMEM),
                  pl.BlockSpec((tm, D), lambda i, ps: (i, 0)),
                  pl.BlockSpec((tm, LANES), lambda i, ps: (i, 0)),
                  pl.BlockSpec((1, D), lambda i, ps: (0, 0)),
                  pl.BlockSpec(memory_space=pl.ANY)],
        out_specs=pl.BlockSpec((tm, D), lambda i, ps: (i, 0)),
        scratch_shapes=[pltpu.VMEM((EXPERT_TOPK, tm, D), F32), pltpu.SemaphoreType.DMA(())],
    )
    return pl.pallas_call(
        _combine_kernel,
        grid_spec=grid_spec,
        out_shape=jax.ShapeDtypeStruct((T, D), F32),
        compiler_params=_cparams(("arbitrary",)),
        name="combine",
    )(pstart, e8, r8, h2, gates, gf, ys)


def _pad_lanes(a, width=LANES, value=0.0):
    pad = [(0, 0)] * (a.ndim - 1) + [(0, width - a.shape[-1])]
    return jnp.pad(a, pad, constant_values=value)


def _layer(h2d, mem2d, batch, seq, mem_len, g_mix, w_in, a_re, a_im, log_dt, b_re, b_im, c_re, c_im, d_skip,
           w_glu, w_proj_a, w_out, g_xattn, g_mem, w_xq, w_xkv, w_xo, g_moe, w_router, b_router,
           w_gu, b_gu, w_dn, b_dn, g_final):
    T, D = h2d.shape
    aw = ATT_HEADS * ATT_HEAD_DIM
    G = a_re.shape[0]
    sw = G * SSM_GROUP

    wq, wk, wv, wu, wga, wgb = jnp.split(w_in, [aw, 2 * aw, 3 * aw, 3 * aw + sw, 3 * aw + sw + D], axis=1)
    wg_t = jnp.concatenate([wga, wgb, wu], axis=1).T.astype(BF16)
    wqv_t = jnp.concatenate([wq * (ATT_HEAD_DIM ** -0.5), wv], axis=1).T.astype(BF16)
    wk_pad = _pad_lanes(wk.reshape(D, ATT_HEADS, ATT_HEAD_DIM)).reshape(D, ATT_HEADS * LANES).astype(BF16)

    zt, q4, v4, kpad, km = _in_proj(h2d, g_mix.reshape(1, D), wg_t, wqv_t, wk_pad)
    nb = seq // MOBA_BLOCK
    att4 = _moba(q4, v4, kpad.reshape(T // MOBA_BLOCK, MOBA_BLOCK, ATT_HEADS * LANES),
                 km.reshape(batch, nb, ATT_HEADS * LANES), batch)

    row = lambda a, v=0.0: _pad_lanes(a, value=v).reshape(G, 1, LANES)
    col = lambda a, v=0.0: _pad_lanes(a, value=v).reshape(G, LANES, 1)
    ssm_params = (
        row(a_re, -1.0), row(a_im), log_dt.reshape(G, 1, 1), col(a_re, -1.0), col(a_im),
        _pad_lanes(jnp.swapaxes(b_re, 1, 2)), _pad_lanes(jnp.swapaxes(b_im, 1, 2)),
        _pad_lanes(c_re), _pad_lanes(c_im),
        _pad_lanes(c_re)[..., None], _pad_lanes(c_im)[..., None],
        d_skip.reshape(G, SSM_GROUP, 1, 1),
    )
    ys3 = _ssm(zt.reshape(zt.shape[0], T // SSM_CHUNK, SSM_CHUNK), 2 * D, ssm_params, batch)
    h1 = _mixer_out(zt, ys3.reshape(sw, T), att4, h2d, w_glu.T.astype(BF16), w_proj_a.T.astype(BF16),
                    w_out.astype(BF16))

    kv = _memkv(mem2d, g_mem.reshape(1, D), w_xkv.astype(BF16), mem_len)
    h2 = _xattn(h1, g_xattn.reshape(1, D), w_xq.astype(BF16), kv, w_xo.astype(BF16), seq, mem_len)

    xn, e8, r8, gates, cnt = _router(h2, g_moe.reshape(1, D), w_router.T, b_router.reshape(N_EXPERTS, 1))
    counts = cnt[:, 0].astype(I32)
    nblk_e = (counts + MOE_BLOCK - 1) // MOE_BLOCK
    bends = jnp.cumsum(nblk_e)
    pstart = ((bends - nblk_e) * MOE_BLOCK).astype(I32)
    nblk = (T * EXPERT_TOPK) // MOE_BLOCK + N_EXPERTS
    blk_expert = jnp.minimum(jnp.sum(bends[None, :] <= jnp.arange(nblk, dtype=I32)[:, None], axis=1),
                             N_EXPERTS - 1).astype(I32)
    nused = bends[-1:].astype(I32)
    blk_expert = jnp.where(jnp.arange(nblk) < nused[0], blk_expert, blk_expert[jnp.maximum(nused[0] - 1, 0)])

    tail = nblk - 1 - jnp.arange(N_EXPERTS, dtype=I32)
    zrow = jnp.concatenate([jnp.where(nblk_e > 0, (bends - 1) * MOE_BLOCK, -1),
                            jnp.where(tail >= nused[0], tail * MOE_BLOCK, -1)]).astype(I32)
    xs = _dispatch(pstart, zrow, e8, r8, xn, nblk * MOE_BLOCK)
    ys = _experts(blk_expert, nused, xs, w_gu, b_gu.reshape(N_EXPERTS, 1, -1), w_dn, b_dn.reshape(N_EXPERTS, 1, -1))
    return _combine(pstart, e8, r8, h2, gates, g_final.reshape(1, D), ys)


def kernel(x, mem, g_mix, w_in, a_re, a_im, log_dt, b_re, b_im, c_re, c_im, d_skip, w_glu, w_proj_a, w_out,
           g_xattn, g_mem, w_xq, w_xkv, w_xo, g_moe, w_router, b_router, w_gu, b_gu, w_dn, b_dn, g_final):
    B, S, D = x.shape
    M = mem.shape[1]
    assert g_mix.shape[0] == 1, "single layer"
    out = _layer(x.reshape(B * S, D), mem.reshape(B * M, D), B, S, M,
                 g_mix[0], w_in[0], a_re[0], a_im[0], log_dt[0], b_re[0], b_im[0], c_re[0], c_im[0], d_skip[0],
                 w_glu[0], w_proj_a[0], w_out[0], g_xattn[0], g_mem[0], w_xq[0], w_xkv[0], w_xo[0],
                 g_moe[0], w_router[0], b_router[0], w_gu[0], b_gu[0], w_dn[0], b_dn[0], g_final)
    return out.reshape(B, S, D)
```

```python
import functools
import math

import jax
import jax.numpy as jnp
from jax import lax
from jax.experimental import pallas as pl
from jax.experimental.pallas import tpu as pltpu

F32 = jnp.float32
BF16 = jnp.bfloat16
I32 = jnp.int32

RMS_EPS = 1e-6
NEG_INF = -1e30
ATT_HEADS = 8
ATT_HEAD_DIM = 64
MOBA_BLOCK = 256
MOBA_TOPK = 3
MOBA_HEADS_PER_STEP = 4
SSM_GROUP = 16
SSM_STATE = 64
SSM_CHUNK = 128
XATTN_HEADS = 4
XATTN_HEAD_DIM = 128
N_EXPERTS = 32
EXPERT_TOPK = 4
SWIGLU_LIMIT = 7.0
SWIGLU_ALPHA = 1.702
MOE_BLOCK = 256
LANES = 128
VMEM_LIMIT = 56 * 1024 * 1024


def _cparams(sem):
    return pltpu.CompilerParams(dimension_semantics=sem, vmem_limit_bytes=VMEM_LIMIT)


def _rmsnorm(x, g):
    return x * lax.rsqrt(jnp.mean(x * x, axis=-1, keepdims=True) + RMS_EPS) * g


def _dot(a, b):
    return jnp.dot(a, b, preferred_element_type=F32)


def _dot_nt(a, b):
    return lax.dot_general(a, b, (((1,), (1,)), ((), ())), preferred_element_type=F32)


def _dot_tn(a, b):
    return lax.dot_general(a, b, (((0,), (0,)), ((), ())), preferred_element_type=F32)


def _split3(x):
    a = x.astype(BF16)
    r = x - a.astype(F32)
    b = r.astype(BF16)
    c = (r - b.astype(F32)).astype(BF16)
    return a, b, c


ROW_TILE = 8


def _to_row_tiles(ref, x, first=0):
    for s in range(ROW_TILE):
        ref[pl.ds(first * ROW_TILE + s, x.shape[0], stride=ROW_TILE), :] = x[:, s * LANES:(s + 1) * LANES]


def _from_row_tiles(ref, rows, first=0):
    return jnp.concatenate(
        [ref[pl.ds(first * ROW_TILE + s, rows, stride=ROW_TILE), :] for s in range(ROW_TILE)], axis=1)


def _tile(ref, r):
    return ref.at[pl.ds(pl.multiple_of(r * ROW_TILE, ROW_TILE), ROW_TILE)]


def _inproj_kernel(x_ref, g_ref, wg_ref, wqv_ref, wk_ref, zt_ref, q_ref, v_ref, k_ref, km_ref):
    n = _rmsnorm(x_ref[...], g_ref[...]).astype(BF16)
    tm = n.shape[0]
    nblk = tm // MOBA_BLOCK
    rows = 512
    for c in range(wg_ref.shape[0] // rows):
        zt_ref[c * rows:(c + 1) * rows, :] = _dot_nt(wg_ref[c * rows:(c + 1) * rows, :], n).astype(BF16)
    for c, dst in ((0, q_ref), (1, v_ref)):
        r = _dot_nt(wqv_ref[c * rows:(c + 1) * rows, :], n).astype(BF16)
        for h in range(ATT_HEADS):
            for j in range(nblk):
                dst[h, j] = r[h * ATT_HEAD_DIM:(h + 1) * ATT_HEAD_DIM, j * MOBA_BLOCK:(j + 1) * MOBA_BLOCK]
    k = _dot(n, wk_ref[...])
    lane = lax.broadcasted_iota(I32, (1, k.shape[1]), 1) & (LANES - 1)
    k = k + jnp.where(lane == ATT_HEAD_DIM, 1.0, 0.0)
    k_ref[...] = k.astype(BF16)
    km_ref[0] = jnp.mean(k.reshape(nblk, MOBA_BLOCK, k.shape[1]), axis=1)


def _in_proj(x2, g, wg_t, wqv_t, wk_pad, tm=512):
    T, D = x2.shape
    nblk = tm // MOBA_BLOCK
    ng = wg_t.shape[0]
    kw = wk_pad.shape[1]
    hd = (ATT_HEADS, T // MOBA_BLOCK, ATT_HEAD_DIM, MOBA_BLOCK)
    return pl.pallas_call(
        _inproj_kernel,
        grid=(T // tm,),
        in_specs=[
            pl.BlockSpec((tm, D), lambda i: (i, 0)),
            pl.BlockSpec((1, D), lambda i: (0, 0)),
            pl.BlockSpec((ng, D), lambda i: (0, 0)),
            pl.BlockSpec(wqv_t.shape, lambda i: (0, 0)),
            pl.BlockSpec((D, kw), lambda i: (0, 0)),
        ],
        out_specs=[
            pl.BlockSpec((ng, tm), lambda i: (0, i)),
            pl.BlockSpec((ATT_HEADS, nblk, ATT_HEAD_DIM, MOBA_BLOCK), lambda i: (0, i, 0, 0)),
            pl.BlockSpec((ATT_HEADS, nblk, ATT_HEAD_DIM, MOBA_BLOCK), lambda i: (0, i, 0, 0)),
            pl.BlockSpec((tm, kw), lambda i: (i, 0)),
            pl.BlockSpec((1, nblk, kw), lambda i: (i, 0, 0)),
        ],
        out_shape=[
            jax.ShapeDtypeStruct((ng, T), BF16),
            jax.ShapeDtypeStruct(hd, BF16),
            jax.ShapeDtypeStruct(hd, BF16),
            jax.ShapeDtypeStruct((T, kw), BF16),
            jax.ShapeDtypeStruct((T // tm, nblk, kw), F32),
        ],
        compiler_params=_cparams(("arbitrary",)),
        name="in_proj",
    )(x2, g, wg_t, wqv_t, wk_pad)


def _moba_kernel(q_ref, v_ref, k_ref, km_ref, o_ref, sel_ref, acc_ref, sa_ref, sb_ref, pb_ref, *, nb, hb):
    blk = MOBA_BLOCK
    heads = range(hb)
    lanes = lambda h: slice(h * LANES, (h + 1) * LANES)
    kms = [_split3(km_ref[0][:, lanes(h)]) for h in heads]
    zpad = jnp.zeros((LANES - ATT_HEAD_DIM, blk), BF16)
    bidx = lax.broadcasted_iota(I32, (nb, blk), 0)
    causal = lax.broadcasted_iota(I32, (blk, blk), 0) <= lax.broadcasted_iota(I32, (blk, blk), 1)
    topk = min(MOBA_TOPK, nb)

    row16 = lax.broadcasted_iota(I32, (16, blk), 0) == 0
    ones16 = jnp.where(row16, 1.0, 0.0).astype(BF16)
    zpad48 = jnp.zeros((LANES - ATT_HEAD_DIM - 16, blk), BF16)

    def qblock(i, carry):
        elig = bidx < i
        q64 = [q_ref[h, i] for h in heads]
        for h in heads:
            q = jnp.concatenate([q64[h], zpad], axis=0)
            gate = (_dot(kms[h][0], q) + _dot(kms[h][1], q)) + _dot(kms[h][2], q)
            g = jnp.where(elig, gate, NEG_INF)
            sel = jnp.zeros((nb, blk), F32)
            for _ in range(topk):
                mx = jnp.max(g, axis=0, keepdims=True)
                first = jnp.min(jnp.where(g == mx, bidx, nb), axis=0, keepdims=True)
                pick = bidx == first
                sel = jnp.where(pick, 1.0, sel)
                g = jnp.where(pick, -3e38, g)
            sel_ref[h] = jnp.where(elig, sel, 0.0)
            acc_ref[h] = jnp.zeros((ATT_HEAD_DIM + 16, blk), F32)
            pb_ref[h] = jnp.zeros((blk, blk), BF16)

        def scores(h, t):
            bias = (sel_ref[h, pl.ds(t, 1), :] - 1.0) * 1e30
            q = jnp.concatenate([q64[h], jnp.where(row16, bias, 0.0).astype(BF16), zpad48], axis=0)
            return _dot(k_ref[t][:, lanes(h)], q)

        def values(h, t, p):
            return _dot(jnp.concatenate([v_ref[h, t], ones16], axis=0), p)

        def softmax(s, m):
            m_new = jnp.maximum(m, jnp.max(s, axis=0, keepdims=True))
            return m_new, jnp.exp(m - m_new), jnp.exp(s - m_new).astype(BF16)

        for h in heads:
            sa_ref[h] = scores(h, 0)

        def kvpair(u, c):
            ta, tb, tn = 2 * u, 2 * u + 1, jnp.minimum(2 * u + 2, nb - 1)
            pv = [values(h, jnp.maximum(ta - 1, 0), pb_ref[h]) for h in heads]
            sb = [scores(h, tb) for h in heads]
            for h in heads:
                sb_ref[h] = sb[h]
                acc_ref[h] = acc_ref[h] * c[2 * h + 1] + pv[h]
            sm = [softmax(sa_ref[h], c[2 * h]) for h in heads]
            pv = [values(h, ta, sm[h][2]) for h in heads]
            sa = [scores(h, tn) for h in heads]
            out = []
            for h in heads:
                sa_ref[h] = sa[h]
                acc_ref[h] = acc_ref[h] * sm[h][1] + pv[h]
                m_new, alpha, p = softmax(sb_ref[h], sm[h][0])
                pb_ref[h] = p
                out += [m_new, alpha]
            return tuple(out)

        m0 = jnp.full((1, blk), -1e29, F32)
        npair = (i + 1) // 2
        c = lax.fori_loop(0, npair, kvpair, (m0, jnp.ones((1, blk), F32)) * hb)
        pv = [values(h, jnp.maximum(2 * npair - 1, 0), pb_ref[h]) for h in heads]
        kd = k_ref[i]
        own = [_dot(kd[:, lanes(h)], jnp.concatenate([q64[h], zpad], axis=0)) for h in heads]
        sm = [softmax(jnp.where(causal, own[h], NEG_INF), c[2 * h]) for h in heads]
        for h in heads:
            res = (acc_ref[h] * c[2 * h + 1] + pv[h]) * sm[h][1] + values(h, i, sm[h][2])
            o_ref[h, i] = (res[:ATT_HEAD_DIM] / res[ATT_HEAD_DIM:ATT_HEAD_DIM + 1]).astype(BF16)
        return carry

    lax.fori_loop(0, nb, qblock, 0)


def _moba(q4, v4, k3, km3, batch, hb=MOBA_HEADS_PER_STEP):
    H, nblk_total, dh, blk = q4.shape
    nb = nblk_total // batch
    return pl.pallas_call(
        functools.partial(_moba_kernel, nb=nb, hb=hb),
        grid=(batch, H // hb),
        in_specs=[
            pl.BlockSpec((hb, nb, dh, blk), lambda b, h: (h, b, 0, 0)),
            pl.BlockSpec((hb, nb, dh, blk), lambda b, h: (h, b, 0, 0)),
            pl.BlockSpec((nb, blk, hb * LANES), lambda b, h: (b, 0, h)),
            pl.BlockSpec((1, nb, hb * LANES), lambda b, h: (b, 0, h)),
        ],
        out_specs=pl.BlockSpec((hb, nb, dh, blk), lambda b, h: (h, b, 0, 0)),
        out_shape=jax.ShapeDtypeStruct(q4.shape, BF16),
        scratch_shapes=[pltpu.VMEM((hb, nb, blk), F32),
                        pltpu.VMEM((hb, dh + 16, blk), F32),
                        pltpu.VMEM((hb, blk, blk), F32),
                        pltpu.VMEM((hb, blk, blk), F32),
                        pltpu.VMEM((hb, blk, blk), BF16)],
        compiler_params=_cparams(("arbitrary", "arbitrary")),
        name="moba",
    )(q4, v4, k3, km3)


def _ssm_kernel(u_ref, ar_ref, ai_ref, ldt_ref, arc_ref, aic_ref, btr_ref, bti_ref, cr_ref, ci_ref,
                ctr_ref, cti_ref, d_ref, y_ref,
                u2_ref, m_ref, pr_ref, pi_ref, qr_ref, qi_ref, kv_ref, sr_ref, si_ref, xr_ref, xi_ref,
                *, nbatch):
    L = SSM_CHUNK
    C = SSM_GROUP
    N = u_ref.shape[1]
    nchunk = N // nbatch
    hi = lax.Precision.HIGHEST

    dt = jnp.exp(ldt_ref[0])
    are = jnp.minimum(ar_ref[0], -1e-4)
    aim = ai_ref[0]
    ea, th = are * dt, aim * dt
    mag = jnp.exp(ea)
    lbr, lbi = mag * jnp.cos(th), mag * jnp.sin(th)
    den = are * are + aim * aim
    cfr = ((lbr - 1.0) * are + lbi * aim) / den
    cfi = (lbi * are - (lbr - 1.0) * aim) / den
    btr, bti = btr_ref[0], bti_ref[0]
    bbr = cfr * btr - cfi * bti
    bbi = cfr * bti + cfi * btr
    crr, cri = cr_ref[0], ci_ref[0]

    eac = jnp.minimum(arc_ref[0], -1e-4) * dt
    thc = aic_ref[0] * dt
    tau = lax.broadcasted_iota(I32, (1, L), 1).astype(F32)
    m0 = jnp.exp(eac * tau)
    pw0r, pw0i = m0 * jnp.cos(thc * tau), m0 * jnp.sin(thc * tau)
    m1 = jnp.exp(eac * (tau + 1.0))
    pw1r, pw1i = m1 * jnp.cos(thc * (tau + 1.0)), m1 * jnp.sin(thc * (tau + 1.0))
    back = (L - 1.0) - lax.broadcasted_iota(I32, (L, 1), 0).astype(F32)
    mb = jnp.exp(ea * back)
    pbr, pbi = mb * jnp.cos(th * back), mb * jnp.sin(th * back)
    ml = jnp.exp(ea * float(L))
    alr, ali = ml * jnp.cos(th * float(L)), ml * jnp.sin(th * float(L))

    upper = lax.broadcasted_iota(I32, (L, L), 1) >= lax.broadcasted_iota(I32, (L, L), 0)
    for ci in range(C):
        cbr = bbr[ci:ci + 1, :] * crr - bbi[ci:ci + 1, :] * cri
        cbi = bbr[ci:ci + 1, :] * cri + bbi[ci:ci + 1, :] * crr
        kv_ref[ci * C:(ci + 1) * C, :] = (jnp.dot(cbr, pw0r, precision=hi, preferred_element_type=F32)
                                          - jnp.dot(cbi, pw0i, precision=hi, preferred_element_type=F32))
        pr_ref[ci * L:(ci + 1) * L, :] = (pbr * bbr[ci:ci + 1, :] - pbi * bbi[ci:ci + 1, :]).astype(BF16)
        pi_ref[ci * L:(ci + 1) * L, :] = (pbr * bbi[ci:ci + 1, :] + pbi * bbr[ci:ci + 1, :]).astype(BF16)
        u2_ref[:, ci * L:(ci + 1) * L] = u_ref[ci]
    for co in range(C):
        cc_r, cc_i = ctr_ref[0, co], cti_ref[0, co]
        qr_ref[:, co * L:(co + 1) * L] = (cc_r * pw1r - cc_i * pw1i).astype(BF16)
        qi_ref[:, co * L:(co + 1) * L] = (-(cc_r * pw1i + cc_i * pw1r)).astype(BF16)

    def toeplitz(ci, carry):
        for co in range(C):
            taps = kv_ref[pl.ds(ci * C + co, 1), :]
            t = pltpu.roll(jnp.broadcast_to(taps, (L, L)), 0, 1, stride=1, stride_axis=0)
            m_ref[pl.ds(pl.multiple_of(ci * L, L), L), co * L:(co + 1) * L] = jnp.where(upper, t, 0.0).astype(BF16)
        return carry

    lax.fori_loop(0, C, toeplitz, 0)

    u2 = u2_ref[...]
    sr_ref[...] = _dot(u2, pr_ref[...]).reshape(nbatch, nchunk, LANES)
    si_ref[...] = _dot(u2, pi_ref[...]).reshape(nbatch, nchunk, LANES)

    def chunk_scan(c, carry):
        xr, xi = carry
        xr_ref[:, pl.ds(c, 1), :] = xr
        xi_ref[:, pl.ds(c, 1), :] = xi
        nr = alr * xr - ali * xi + sr_ref[:, pl.ds(c, 1), :]
        ni = alr * xi + ali * xr + si_ref[:, pl.ds(c, 1), :]
        return nr, ni

    zero = jnp.zeros((nbatch, 1, LANES), F32)
    lax.fori_loop(0, nchunk, chunk_scan, (zero, zero))
    xpr = xr_ref[...].reshape(N, LANES).astype(BF16)
    xpi = xi_ref[...].reshape(N, LANES).astype(BF16)

    wide = 4 * L
    for n0 in range(C * L // wide):
        y = (_dot(u2, m_ref[:, n0 * wide:(n0 + 1) * wide])
             + _dot(xpr, qr_ref[:, n0 * wide:(n0 + 1) * wide])
             + _dot(xpi, qi_ref[:, n0 * wide:(n0 + 1) * wide]))
        for cc in range(wide // L):
            co = n0 * (wide // L) + cc
            y_ref[co] = (y[:, cc * L:(cc + 1) * L] + d_ref[0, co] * u_ref[co].astype(F32)).astype(BF16)


def _ssm(zt3, row0, params, nbatch):
    (ar, ai, ldt, arc, aic, btr, bti, cr, ci, ctr, cti, d) = params
    G = ar.shape[0]
    C, L = SSM_GROUP, SSM_CHUNK
    N = zt3.shape[1]
    g3 = lambda g: (g, 0, 0)
    g4 = lambda g: (g, 0, 0, 0)
    return pl.pallas_call(
        functools.partial(_ssm_kernel, nbatch=nbatch),
        grid=(G,),
        in_specs=[
            pl.BlockSpec((C, N, L), lambda g: (row0 // C + g, 0, 0)),
            pl.BlockSpec((1, 1, LANES), g3), pl.BlockSpec((1, 1, LANES), g3), pl.BlockSpec((1, 1, 1), g3),
            pl.BlockSpec((1, LANES, 1), g3), pl.BlockSpec((1, LANES, 1), g3),
            pl.BlockSpec((1, C, LANES), g3), pl.BlockSpec((1, C, LANES), g3),
            pl.BlockSpec((1, C, LANES), g3), pl.BlockSpec((1, C, LANES), g3),
            pl.BlockSpec((1, C, LANES, 1), g4), pl.BlockSpec((1, C, LANES, 1), g4),
            pl.BlockSpec((1, C, 1, 1), g4),
        ],
        out_specs=pl.BlockSpec((C, N, L), g3),
        out_shape=jax.ShapeDtypeStruct((G * C, N, L), BF16),
        scratch_shapes=[
            pltpu.VMEM((N, C * L), BF16),
            pltpu.VMEM((C * L, C * L), BF16),
            pltpu.VMEM((C * L, LANES), BF16), pltpu.VMEM((C * L, LANES), BF16),
            pltpu.VMEM((LANES, C * L), BF16), pltpu.VMEM((LANES, C * L), BF16),
            pltpu.VMEM((C * C, L), F32),
            pltpu.VMEM((nbatch, N // nbatch, LANES), F32), pltpu.VMEM((nbatch, N // nbatch, LANES), F32),
            pltpu.VMEM((nbatch, N // nbatch, LANES), F32), pltpu.VMEM((nbatch, N // nbatch, LANES), F32),
        ],
        compiler_params=_cparams(("arbitrary",)),
        name="ssm",
    )(zt3, ar, ai, ldt, arc, aic, btr, bti, cr, ci, ctr, cti, d)


def _mixer_out_kernel(ga_ref, gb_ref, ys_ref, at_ref, x_ref, wglu_ref, wpa_ref, wout_ref, h_ref):
    dm = ga_ref.shape[0]
    nblk = at_ref.shape[1]
    att = jnp.concatenate(
        [jnp.concatenate([at_ref[h, j] for j in range(nblk)], axis=1) for h in range(ATT_HEADS)], axis=0)
    ya = _dot(wpa_ref[...], att)
    gl = jax.nn.gelu(ys_ref[...].astype(F32)).astype(BF16)
    zb = _dot(wglu_ref[...], gl)
    yb = zb[:dm] * jax.nn.sigmoid(zb[dm:])
    merged = jax.nn.sigmoid(ga_ref[...].astype(F32)) * ya + jax.nn.sigmoid(gb_ref[...].astype(F32)) * yb
    h_ref[...] = x_ref[...] + _dot_tn(merged.astype(BF16), wout_ref[...])


def _mixer_out(zt, ys_t, att4, x2, wglu_t, wpa_t, wout, tm=512):
    T, D = x2.shape
    nblk = tm // MOBA_BLOCK
    sw = ys_t.shape[0]
    full = lambda a: pl.BlockSpec(a.shape, lambda i: (0,) * a.ndim)
    return pl.pallas_call(
        _mixer_out_kernel,
        grid=(T // tm,),
        in_specs=[
            pl.BlockSpec((D, tm), lambda i: (0, i)),
            pl.BlockSpec((D, tm), lambda i: (1, i)),
            pl.BlockSpec((sw, tm), lambda i: (0, i)),
            pl.BlockSpec((ATT_HEADS, nblk, ATT_HEAD_DIM, MOBA_BLOCK), lambda i: (0, i, 0, 0)),
            pl.BlockSpec((tm, D), lambda i: (i, 0)),
            full(wglu_t), full(wpa_t), full(wout),
        ],
        out_specs=pl.BlockSpec((tm, D), lambda i: (i, 0)),
        out_shape=jax.ShapeDtypeStruct((T, D), F32),
        compiler_params=_cparams(("arbitrary",)),
        name="mixer_out",
    )(zt, zt, ys_t, att4, x2, wglu_t, wpa_t, wout)


def _memkv_kernel(mem_ref, g_ref, w_ref, kv_ref):
    kv_ref[...] = _dot(_rmsnorm(mem_ref[...], g_ref[...]).astype(BF16), w_ref[...]).astype(BF16)


def _memkv(mem2, g, w, rows):
    M, D = mem2.shape
    return pl.pallas_call(
        _memkv_kernel,
        grid=(M // rows,),
        in_specs=[pl.BlockSpec((rows, D), lambda i: (i, 0)), pl.BlockSpec((1, D), lambda i: (0, 0)),
                  pl.BlockSpec(w.shape, lambda i: (0, 0))],
        out_specs=pl.BlockSpec((rows, w.shape[1]), lambda i: (i, 0)),
        out_shape=jax.ShapeDtypeStruct((M, w.shape[1]), BF16),
        compiler_params=_cparams(("arbitrary",)),
        name="memkv",
    )(mem2, g, w)


def _xattn_kernel(h_ref, g_ref, wq_ref, kv_ref, wo_ref, o_ref):
    h = h_ref[...]
    q = _dot(_rmsnorm(h, g_ref[...]).astype(BF16), wq_ref[...]).astype(BF16)
    kv = kv_ref[...]
    xw = XATTN_HEADS * XATTN_HEAD_DIM
    scale = XATTN_HEAD_DIM ** -0.5
    outs = []
    for hd in range(XATTN_HEADS):
        sl = slice(hd * XATTN_HEAD_DIM, (hd + 1) * XATTN_HEAD_DIM)
        s = _dot_nt(q[:, sl], kv[:, sl]) * scale
        p = jnp.exp(s - jnp.max(s, axis=-1, keepdims=True))
        p = p / jnp.sum(p, axis=-1, keepdims=True)
        outs.append(_dot(p.astype(BF16), kv[:, xw + hd * XATTN_HEAD_DIM:xw + (hd + 1) * XATTN_HEAD_DIM]))
    o = jnp.concatenate(outs, axis=1).astype(BF16)
    o_ref[...] = h + _dot(o, wo_ref[...])


def _xattn(h1, g, wq, kv, wo, seq, mem_len, tm=512):
    T, D = h1.shape
    per_b = seq // tm
    return pl.pallas_call(
        _xattn_kernel,
        grid=(T // tm,),
        in_specs=[
            pl.BlockSpec((tm, D), lambda i: (i, 0)),
            pl.BlockSpec((1, D), lambda i: (0, 0)),
            pl.BlockSpec(wq.shape, lambda i: (0, 0)),
            pl.BlockSpec((mem_len, kv.shape[1]), lambda i: (i // per_b, 0)),
            pl.BlockSpec(wo.shape, lambda i: (0, 0)),
        ],
        out_specs=pl.BlockSpec((tm, D), lambda i: (i, 0)),
        out_shape=jax.ShapeDtypeStruct((T, D), F32),
        compiler_params=_cparams(("arbitrary",)),
        name="xattn",
    )(h1, g, wq, kv, wo)


def _router_kernel(h_ref, g_ref, wr_ref, br_ref, xn_ref, e_ref, r_ref, gate_ref, cnt_ref, carry_ref):
    i = pl.program_id(0)
    E = N_EXPERTS
    tm = h_ref.shape[0]

    @pl.when(i == 0)
    def _():
        carry_ref[...] = jnp.zeros_like(carry_ref)

    n = _rmsnorm(h_ref[...], g_ref[...])
    _to_row_tiles(xn_ref, n)
    na, nb_, nc = _split3(n)
    wa, wb, wc = _split3(wr_ref[...])
    logits = (_dot_nt(wa, na) + (_dot_nt(wa, nb_) + _dot_nt(wb, na))
              + (_dot_nt(wb, nb_) + _dot_nt(wa, nc) + _dot_nt(wc, na))) + br_ref[...]
    eidx = lax.broadcasted_iota(I32, (E, tm), 0)
    g = logits
    picks, vals = [], []
    for _ in range(EXPERT_TOPK):
        mx = jnp.max(g, axis=0, keepdims=True)
        first = jnp.min(jnp.where(g == mx, eidx, E), axis=0, keepdims=True)
        pick = eidx == first
        picks.append(pick)
        vals.append(mx)
        g = jnp.where(pick, -3e38, g)
    ex = [jnp.exp(v - vals[0]) for v in vals]
    tot = ex[0] + ex[1] + ex[2] + ex[3]
    sel = jnp.zeros((E, tm), F32)
    for pk in picks:
        sel = jnp.where(pk, 1.0, sel)
    before = (lax.broadcasted_iota(I32, (tm, tm), 0) < lax.broadcasted_iota(I32, (tm, tm), 1))
    prefix = _dot(sel.astype(BF16), jnp.where(before, 1.0, 0.0).astype(BF16))
    pos = prefix + carry_ref[:, 0:1]
    zi = jnp.zeros((8 - EXPERT_TOPK, tm), I32)
    e_rows = [jnp.sum(jnp.where(pk, eidx, 0), axis=0, keepdims=True) for pk in picks]
    r_rows = [jnp.sum(jnp.where(pk, pos, 0.0), axis=0, keepdims=True).astype(I32) for pk in picks]
    e_ref[...] = jnp.concatenate(e_rows + [zi], axis=0)
    r_ref[...] = jnp.concatenate(r_rows + [zi], axis=0)
    gate_rows = jnp.concatenate([x / tot for x in ex] + [jnp.zeros((LANES - EXPERT_TOPK, tm), F32)], axis=0)
    gate_ref[...] = gate_rows.T
    carry_ref[...] = carry_ref[...] + jnp.sum(sel, axis=1, keepdims=True)
    cnt_ref[...] = carry_ref[...]


def _router(h2, g, wr_t, br, tm=256):
    T, D = h2.shape
    E = N_EXPERTS
    return pl.pallas_call(
        _router_kernel,
        grid=(T // tm,),
        in_specs=[pl.BlockSpec((tm, D), lambda i: (i, 0)), pl.BlockSpec((1, D), lambda i: (0, 0)),
                  pl.BlockSpec((E, D), lambda i: (0, 0)), pl.BlockSpec((E, 1), lambda i: (0, 0))],
        out_specs=[pl.BlockSpec((tm * ROW_TILE, LANES), lambda i: (i, 0)),
                   pl.BlockSpec((8, tm), lambda i: (0, i)), pl.BlockSpec((8, tm), lambda i: (0, i)),
                   pl.BlockSpec((tm, LANES), lambda i: (i, 0)),
                   pl.BlockSpec((E, LANES), lambda i: (0, 0))],
        out_shape=[jax.ShapeDtypeStruct((T * ROW_TILE, LANES), F32),
                   jax.ShapeDtypeStruct((8, T), I32), jax.ShapeDtypeStruct((8, T), I32),
                   jax.ShapeDtypeStruct((T, LANES), F32),
                   jax.ShapeDtypeStruct((E, LANES), F32)],
        scratch_shapes=[pltpu.VMEM((E, LANES), F32)],
        compiler_params=_cparams(("arbitrary",)),
        name="router",
    )(h2, g, wr_t, br)


def _positions_kernel(ps_ref, e_ref, r_ref, pos_ref):
    e = e_ref[...]
    pos = r_ref[...]
    for x in range(N_EXPERTS):
        pos = pos + jnp.where(e == x, ps_ref[x], 0)
    for c in range(pos_ref.shape[0]):
        pos_ref[c] = pos[:, c * LANES:(c + 1) * LANES]


def _positions(pstart, e8, r8, tm=2048):
    T = e8.shape[1]
    grid_spec = pltpu.PrefetchScalarGridSpec(
        num_scalar_prefetch=1,
        grid=(T // tm,),
        in_specs=[pl.BlockSpec((8, tm), lambda i, ps: (0, i)), pl.BlockSpec((8, tm), lambda i, ps: (0, i))],
        out_specs=pl.BlockSpec((tm // LANES, 8, LANES), lambda i, ps: (i, 0, 0)),
    )
    return pl.pallas_call(
        _positions_kernel, grid_spec=grid_spec, out_shape=jax.ShapeDtypeStruct((T // LANES, 8, LANES), I32),
        compiler_params=_cparams(("arbitrary",)), name="positions",
    )(pstart, e8, r8)


def _row_dmas(tm, make_copy):
    def sweep(op):
        for k in range(EXPERT_TOPK):
            for c in range(tm // LANES):
                def body(lane, carry, k=k, c=c):
                    op(make_copy(c, k, lane), k)
                    return carry
                lax.fori_loop(0, LANES, body, 0, unroll=8)

    sweep(lambda cp, k: cp.start(priority=k % 2))
    sweep(lambda cp, k: cp.wait())


def _dispatch_kernel(zr_ref, pos_ref, xn_ref, xs_ref, zbuf, sem, zsem):
    tm = xn_ref.shape[0] // ROW_TILE

    @pl.when(pl.program_id(0) == 0)
    def _():
        zbuf[...] = jnp.zeros_like(zbuf)

        def zero_copy(e):
            first = pl.multiple_of(zr_ref[e] * ROW_TILE, MOE_BLOCK * ROW_TILE)
            return pltpu.make_async_copy(zbuf, xs_ref.at[pl.ds(first, MOE_BLOCK * ROW_TILE)], zsem)

        for e in range(2 * N_EXPERTS):
            @pl.when(zr_ref[e] >= 0)
            def _():
                zero_copy(e).start()
        for e in range(2 * N_EXPERTS):
            @pl.when(zr_ref[e] >= 0)
            def _():
                zero_copy(e).wait()

    _row_dmas(tm, lambda c, k, lane: pltpu.make_async_copy(
        _tile(xn_ref, c * LANES + lane), _tile(xs_ref, pos_ref[c, k, lane]), sem))


def _dispatch(zrow, pos3, xn, rows_total, tm=256):
    T = xn.shape[0] // ROW_TILE
    grid_spec = pltpu.PrefetchScalarGridSpec(
        num_scalar_prefetch=1,
        grid=(T // tm,),
        in_specs=[pl.BlockSpec((tm // LANES, 8, LANES), lambda i, zr: (i, 0, 0), memory_space=pltpu.SMEM),
                  pl.BlockSpec((tm * ROW_TILE, LANES), lambda i, zr: (i, 0))],
        out_specs=pl.BlockSpec(memory_space=pl.ANY),
        scratch_shapes=[pltpu.VMEM((MOE_BLOCK * ROW_TILE, LANES), F32), pltpu.SemaphoreType.DMA(()),
                        pltpu.SemaphoreType.DMA(())],
    )
    return pl.pallas_call(
        _dispatch_kernel,
        grid_spec=grid_spec,
        out_shape=jax.ShapeDtypeStruct((rows_total * ROW_TILE, LANES), F32),
        compiler_params=_cparams(("arbitrary",)),
        name="dispatch",
    )(zrow, pos3, xn)


def _experts_kernel(be_ref, nu_ref, xs_ref, wgu_ref, bgu_ref, wdn_ref, bdn_ref, ys_ref, wgu_bf, wdn_bf):
    i = pl.program_id(0)
    dff = wdn_ref.shape[1]

    @pl.when(i < nu_ref[0])
    def _():
        prev = be_ref[jnp.maximum(i - 1, 0)]

        @pl.when((i == 0) | (be_ref[i] != prev))
        def _():
            wgu_bf[...] = wgu_ref[0].astype(BF16)
            wdn_bf[...] = wdn_ref[0].astype(BF16)

        gu = _dot(_from_row_tiles(xs_ref, MOE_BLOCK).astype(BF16), wgu_bf[...]) + bgu_ref[0]
        gate = jnp.minimum(gu[:, :dff], SWIGLU_LIMIT)
        up = jnp.clip(gu[:, dff:], -SWIGLU_LIMIT, SWIGLU_LIMIT)
        hid = (up + 1.0) * gate * jax.nn.sigmoid(SWIGLU_ALPHA * gate)
        _to_row_tiles(ys_ref, _dot(hid.astype(BF16), wdn_bf[...]) + bdn_ref[0])

    @pl.when(i >= nu_ref[0])
    def _():
        ys_ref[...] = jnp.zeros_like(ys_ref)


def _experts(blk_expert, nused, xs, wgu, bgu, wdn, bdn):
    rows = xs.shape[0] // ROW_TILE
    E, D, F2 = wgu.shape
    dff = wdn.shape[1]
    nblk = rows // MOE_BLOCK
    grid_spec = pltpu.PrefetchScalarGridSpec(
        num_scalar_prefetch=2,
        grid=(nblk,),
        in_specs=[pl.BlockSpec((MOE_BLOCK * ROW_TILE, LANES), lambda i, be, nu: (jnp.minimum(i, nu[0] - 1), 0)),
                  pl.BlockSpec((1, D, F2), lambda i, be, nu: (be[i], 0, 0)),
                  pl.BlockSpec((1, 1, F2), lambda i, be, nu: (be[i], 0, 0)),
                  pl.BlockSpec((1, dff, D), lambda i, be, nu: (be[i], 0, 0)),
                  pl.BlockSpec((1, 1, D), lambda i, be, nu: (be[i], 0, 0))],
        out_specs=pl.BlockSpec((MOE_BLOCK * ROW_TILE, LANES), lambda i, be, nu: (i, 0)),
        scratch_shapes=[pltpu.VMEM((D, F2), BF16), pltpu.VMEM((dff, D), BF16)],
    )
    return pl.pallas_call(
        _experts_kernel,
        grid_spec=grid_spec,
        out_shape=jax.ShapeDtypeStruct((rows * ROW_TILE, LANES), F32),
        compiler_params=_cparams(("arbitrary",)),
        name="experts",
    )(blk_expert, nused, xs, wgu, bgu, wdn, bdn)


def _combine_kernel(pos_ref, h_ref, gate_ref, gf_ref, ys_ref, o_ref, buf, sem):
    tm = h_ref.shape[0]
    _row_dmas(tm, lambda c, k, lane: pltpu.make_async_copy(
        _tile(ys_ref, pos_ref[c, k, lane]), _tile(buf, k * tm + c * LANES + lane), sem))
    gates = gate_ref[...]
    acc = h_ref[...]
    for k in range(EXPERT_TOPK):
        acc = acc + gates[:, k:k + 1] * _from_row_tiles(buf, tm, first=k * tm)
    o_ref[...] = _rmsnorm(acc, gf_ref[...])


def _combine(pos3, h2, gates, gf, ys, tm=256):
    T, D = h2.shape
    return pl.pallas_call(
        _combine_kernel,
        grid=(T // tm,),
        in_specs=[pl.BlockSpec((tm // LANES, 8, LANES), lambda i: (i, 0, 0), memory_space=pltpu.SMEM),
                  pl.BlockSpec((tm, D), lambda i: (i, 0)),
                  pl.BlockSpec((tm, LANES), lambda i: (i, 0)),
                  pl.BlockSpec((1, D), lambda i: (0, 0)),
                  pl.BlockSpec(memory_space=pl.ANY)],
        out_specs=pl.BlockSpec((tm, D), lambda i: (i, 0)),
        out_shape=jax.ShapeDtypeStruct((T, D), F32),
        scratch_shapes=[pltpu.VMEM((EXPERT_TOPK * tm * ROW_TILE, LANES), F32), pltpu.SemaphoreType.DMA(())],
        compiler_params=_cparams(("arbitrary",)),
        name="combine",
    )(pos3, h2, gates, gf, ys)


def _pad_lanes(a, width=LANES, value=0.0):
    pad = [(0, 0)] * (a.ndim - 1) + [(0, width - a.shape[-1])]
    return jnp.pad(a, pad, constant_values=value)


def _layer(h2d, mem2d, batch, seq, mem_len, g_mix, w_in, a_re, a_im, log_dt, b_re, b_im, c_re, c_im, d_skip,
           w_glu, w_proj_a, w_out, g_xattn, g_mem, w_xq, w_xkv, w_xo, g_moe, w_router, b_router,
           w_gu, b_gu, w_dn, b_dn, g_final):
    T, D = h2d.shape
    aw = ATT_HEADS * ATT_HEAD_DIM
    G = a_re.shape[0]
    sw = G * SSM_GROUP

    wq, wk, wv, wu, wga, wgb = jnp.split(w_in, [aw, 2 * aw, 3 * aw, 3 * aw + sw, 3 * aw + sw + D], axis=1)
    wg_t = jnp.concatenate([wga, wgb, wu], axis=1).T.astype(BF16)
    wqv_t = jnp.concatenate([wq * (ATT_HEAD_DIM ** -0.5), wv], axis=1).T.astype(BF16)
    wk_pad = _pad_lanes(wk.reshape(D, ATT_HEADS, ATT_HEAD_DIM)).reshape(D, ATT_HEADS * LANES).astype(BF16)

    zt, q4, v4, kpad, km = _in_proj(h2d, g_mix.reshape(1, D), wg_t, wqv_t, wk_pad)
    nb = seq // MOBA_BLOCK
    att4 = _moba(q4, v4, kpad.reshape(T // MOBA_BLOCK, MOBA_BLOCK, ATT_HEADS * LANES),
                 km.reshape(batch, nb, ATT_HEADS * LANES), batch)

    row = lambda a, v=0.0: _pad_lanes(a, value=v).reshape(G, 1, LANES)
    col = lambda a, v=0.0: _pad_lanes(a, value=v).reshape(G, LANES, 1)
    ssm_params = (
        row(a_re, -1.0), row(a_im), log_dt.reshape(G, 1, 1), col(a_re, -1.0), col(a_im),
        _pad_lanes(jnp.swapaxes(b_re, 1, 2)), _pad_lanes(jnp.swapaxes(b_im, 1, 2)),
        _pad_lanes(c_re), _pad_lanes(c_im),
        _pad_lanes(c_re)[..., None], _pad_lanes(c_im)[..., None],
        d_skip.reshape(G, SSM_GROUP, 1, 1),
    )
    ys3 = _ssm(zt.reshape(zt.shape[0], T // SSM_CHUNK, SSM_CHUNK), 2 * D, ssm_params, batch)
    h1 = _mixer_out(zt, ys3.reshape(sw, T), att4, h2d, w_glu.T.astype(BF16), w_proj_a.T.astype(BF16),
                    w_out.astype(BF16))

    kv = _memkv(mem2d, g_mem.reshape(1, D), w_xkv.astype(BF16), mem_len)
    h2 = _xattn(h1, g_xattn.reshape(1, D), w_xq.astype(BF16), kv, w_xo.astype(BF16), seq, mem_len)

    xn, e8, r8, gates, cnt = _router(h2, g_moe.reshape(1, D), w_router.T, b_router.reshape(N_EXPERTS, 1))
    counts = cnt[:, 0].astype(I32)
    nblk_e = (counts + MOE_BLOCK - 1) // MOE_BLOCK
    bends = jnp.cumsum(nblk_e)
    pstart = ((bends - nblk_e) * MOE_BLOCK).astype(I32)
    nblk = (T * EXPERT_TOPK) // MOE_BLOCK + N_EXPERTS
    blk_expert = jnp.minimum(jnp.sum(bends[None, :] <= jnp.arange(nblk, dtype=I32)[:, None], axis=1),
                             N_EXPERTS - 1).astype(I32)
    nused = bends[-1:].astype(I32)
    blk_expert = jnp.where(jnp.arange(nblk) < nused[0], blk_expert, blk_expert[jnp.maximum(nused[0] - 1, 0)])

    tail = nblk - 1 - jnp.arange(N_EXPERTS, dtype=I32)
    zrow = jnp.concatenate([jnp.where(nblk_e > 0, (bends - 1) * MOE_BLOCK, -1),
                            jnp.where(tail >= nused[0], tail * MOE_BLOCK, -1)]).astype(I32)
    pos3 = _positions(pstart, e8, r8)
    xs = _dispatch(zrow, pos3, xn, nblk * MOE_BLOCK)
    ys = _experts(blk_expert, nused, xs, w_gu, b_gu.reshape(N_EXPERTS, 1, -1), w_dn, b_dn.reshape(N_EXPERTS, 1, -1))
    return _combine(pos3, h2, gates, g_final.reshape(1, D), ys)


def kernel(x, mem, g_mix, w_in, a_re, a_im, log_dt, b_re, b_im, c_re, c_im, d_skip, w_glu, w_proj_a, w_out,
           g_xattn, g_mem, w_xq, w_xkv, w_xo, g_moe, w_router, b_router, w_gu, b_gu, w_dn, b_dn, g_final):
    B, S, D = x.shape
    M = mem.shape[1]
    assert g_mix.shape[0] == 1, "single layer"
    out = _layer(x.reshape(B * S, D), mem.reshape(B * M, D), B, S, M,
                 g_mix[0], w_in[0], a_re[0], a_im[0], log_dt[0], b_re[0], b_im[0], c_re[0], c_im[0], d_skip[0],
                 w_glu[0], w_proj_a[0], w_out[0], g_xattn[0], g_mem[0], w_xq[0], w_xkv[0], w_xo[0],
                 g_moe[0], w_router[0], b_router[0], w_gu[0], b_gu[0], w_dn[0], b_dn[0], g_final)
    return out.reshape(B, S, D)
```

```python
import functools
import math

import jax
import jax.numpy as jnp
from jax import lax
from jax.experimental import pallas as pl
from jax.experimental.pallas import tpu as pltpu

F32 = jnp.float32
BF16 = jnp.bfloat16
I32 = jnp.int32

RMS_EPS = 1e-6
NEG_INF = -1e30
ATT_HEADS = 8
ATT_HEAD_DIM = 64
MOBA_BLOCK = 256
MOBA_TOPK = 3
MOBA_HEADS_PER_STEP = 4
SSM_GROUP = 16
SSM_STATE = 64
SSM_CHUNK = 128
XATTN_HEADS = 4
XATTN_HEAD_DIM = 128
N_EXPERTS = 32
EXPERT_TOPK = 4
SWIGLU_LIMIT = 7.0
SWIGLU_ALPHA = 1.702
MOE_BLOCK = 512
LANES = 128
VMEM_LIMIT = 56 * 1024 * 1024


def _cparams(sem):
    return pltpu.CompilerParams(dimension_semantics=sem, vmem_limit_bytes=VMEM_LIMIT)


def _rmsnorm(x, g):
    return x * lax.rsqrt(jnp.mean(x * x, axis=-1, keepdims=True) + RMS_EPS) * g


def _dot(a, b):
    return jnp.dot(a, b, preferred_element_type=F32)


def _dot_nt(a, b):
    return lax.dot_general(a, b, (((1,), (1,)), ((), ())), preferred_element_type=F32)


def _dot_tn(a, b):
    return lax.dot_general(a, b, (((0,), (0,)), ((), ())), preferred_element_type=F32)


def _split3(x):
    a = x.astype(BF16)
    r = x - a.astype(F32)
    b = r.astype(BF16)
    c = (r - b.astype(F32)).astype(BF16)
    return a, b, c


ROW_TILE = 8


def _to_row_tiles(ref, x, first=0):
    for s in range(ROW_TILE):
        ref[pl.ds(first * ROW_TILE + s, x.shape[0], stride=ROW_TILE), :] = x[:, s * LANES:(s + 1) * LANES]


def _from_row_tiles(ref, rows, first=0):
    return jnp.concatenate(
        [ref[pl.ds(first * ROW_TILE + s, rows, stride=ROW_TILE), :] for s in range(ROW_TILE)], axis=1)


def _tile(ref, r):
    return ref.at[pl.ds(pl.multiple_of(r * ROW_TILE, ROW_TILE), ROW_TILE)]


def _inproj_kernel(x_ref, g_ref, wg_ref, wqv_ref, wk_ref, zt_ref, q_ref, v_ref, k_ref, km_ref):
    n = _rmsnorm(x_ref[...], g_ref[...]).astype(BF16)
    tm = n.shape[0]
    nblk = tm // MOBA_BLOCK
    rows = 512
    for c in range(wg_ref.shape[0] // rows):
        zt_ref[c * rows:(c + 1) * rows, :] = _dot_nt(wg_ref[c * rows:(c + 1) * rows, :], n).astype(BF16)
    for c, dst in ((0, q_ref), (1, v_ref)):
        r = _dot_nt(wqv_ref[c * rows:(c + 1) * rows, :], n).astype(BF16)
        for h in range(ATT_HEADS):
            for j in range(nblk):
                dst[h, j] = r[h * ATT_HEAD_DIM:(h + 1) * ATT_HEAD_DIM, j * MOBA_BLOCK:(j + 1) * MOBA_BLOCK]
    k = _dot(n, wk_ref[...])
    lane = lax.broadcasted_iota(I32, (1, k.shape[1]), 1) & (LANES - 1)
    k = k + jnp.where(lane == ATT_HEAD_DIM, 1.0, 0.0)
    k_ref[...] = k.astype(BF16)
    km_ref[0] = jnp.mean(k.reshape(nblk, MOBA_BLOCK, k.shape[1]), axis=1)


def _in_proj(x2, g, wg_t, wqv_t, wk_pad, tm=512):
    T, D = x2.shape
    nblk = tm // MOBA_BLOCK
    ng = wg_t.shape[0]
    kw = wk_pad.shape[1]
    hd = (ATT_HEADS, T // MOBA_BLOCK, ATT_HEAD_DIM, MOBA_BLOCK)
    return pl.pallas_call(
        _inproj_kernel,
        grid=(T // tm,),
        in_specs=[
            pl.BlockSpec((tm, D), lambda i: (i, 0)),
            pl.BlockSpec((1, D), lambda i: (0, 0)),
            pl.BlockSpec((ng, D), lambda i: (0, 0)),
            pl.BlockSpec(wqv_t.shape, lambda i: (0, 0)),
            pl.BlockSpec((D, kw), lambda i: (0, 0)),
        ],
        out_specs=[
            pl.BlockSpec((ng, tm), lambda i: (0, i)),
            pl.BlockSpec((ATT_HEADS, nblk, ATT_HEAD_DIM, MOBA_BLOCK), lambda i: (0, i, 0, 0)),
            pl.BlockSpec((ATT_HEADS, nblk, ATT_HEAD_DIM, MOBA_BLOCK), lambda i: (0, i, 0, 0)),
            pl.BlockSpec((tm, kw), lambda i: (i, 0)),
            pl.BlockSpec((1, nblk, kw), lambda i: (i, 0, 0)),
        ],
        out_shape=[
            jax.ShapeDtypeStruct((ng, T), BF16),
            jax.ShapeDtypeStruct(hd, BF16),
            jax.ShapeDtypeStruct(hd, BF16),
            jax.ShapeDtypeStruct((T, kw), BF16),
            jax.ShapeDtypeStruct((T // tm, nblk, kw), F32),
        ],
        compiler_params=_cparams(("arbitrary",)),
        name="in_proj",
    )(x2, g, wg_t, wqv_t, wk_pad)


def _moba_kernel(q_ref, v_ref, k_ref, km_ref, o_ref, sel_ref, acc_ref, sa_ref, sb_ref, pb_ref, *, nb, hb):
    blk = MOBA_BLOCK
    heads = range(hb)
    lanes = lambda h: slice(h * LANES, (h + 1) * LANES)
    kms = [_split3(km_ref[0][:, lanes(h)]) for h in heads]
    zpad = jnp.zeros((LANES - ATT_HEAD_DIM, blk), BF16)
    bidx = lax.broadcasted_iota(I32, (nb, blk), 0)
    causal = lax.broadcasted_iota(I32, (blk, blk), 0) <= lax.broadcasted_iota(I32, (blk, blk), 1)
    topk = min(MOBA_TOPK, nb)

    row16 = lax.broadcasted_iota(I32, (16, blk), 0) == 0
    ones16 = jnp.where(row16, 1.0, 0.0).astype(BF16)
    zpad48 = jnp.zeros((LANES - ATT_HEAD_DIM - 16, blk), BF16)

    def qblock(i, carry):
        elig = bidx < i
        q64 = [q_ref[h, i] for h in heads]
        for h in heads:
            q = jnp.concatenate([q64[h], zpad], axis=0)
            gate = (_dot(kms[h][0], q) + _dot(kms[h][1], q)) + _dot(kms[h][2], q)
            g = jnp.where(elig, gate, NEG_INF)
            sel = jnp.zeros((nb, blk), F32)
            for _ in range(topk):
                mx = jnp.max(g, axis=0, keepdims=True)
                first = jnp.min(jnp.where(g == mx, bidx, nb), axis=0, keepdims=True)
                pick = bidx == first
                sel = jnp.where(pick, 1.0, sel)
                g = jnp.where(pick, -3e38, g)
            sel_ref[h] = jnp.where(elig, sel, 0.0)
            acc_ref[h] = jnp.zeros((ATT_HEAD_DIM + 16, blk), F32)
            pb_ref[h] = jnp.zeros((blk, blk), BF16)

        def scores(h, t):
            bias = (sel_ref[h, pl.ds(t, 1), :] - 1.0) * 1e30
            q = jnp.concatenate([q64[h], jnp.where(row16, bias, 0.0).astype(BF16), zpad48], axis=0)
            return _dot(k_ref[t][:, lanes(h)], q)

        def values(h, t, p):
            return _dot(jnp.concatenate([v_ref[h, t], ones16], axis=0), p)

        def softmax(s, m):
            m_new = jnp.maximum(m, jnp.max(s, axis=0, keepdims=True))
            return m_new, jnp.exp(m - m_new), jnp.exp(s - m_new).astype(BF16)

        for h in heads:
            sa_ref[h] = scores(h, 0)

        def kvpair(u, c):
            ta, tb, tn = 2 * u, 2 * u + 1, jnp.minimum(2 * u + 2, nb - 1)
            pv = [values(h, jnp.maximum(ta - 1, 0), pb_ref[h]) for h in heads]
            sb = [scores(h, tb) for h in heads]
            for h in heads:
                sb_ref[h] = sb[h]
                acc_ref[h] = acc_ref[h] * c[2 * h + 1] + pv[h]
            sm = [softmax(sa_ref[h], c[2 * h]) for h in heads]
            pv = [values(h, ta, sm[h][2]) for h in heads]
            sa = [scores(h, tn) for h in heads]
            out = []
            for h in heads:
                sa_ref[h] = sa[h]
                acc_ref[h] = acc_ref[h] * sm[h][1] + pv[h]
                m_new, alpha, p = softmax(sb_ref[h], sm[h][0])
                pb_ref[h] = p
                out += [m_new, alpha]
            return tuple(out)

        m0 = jnp.full((1, blk), -1e29, F32)
        npair = (i + 1) // 2
        c = lax.fori_loop(0, npair, kvpair, (m0, jnp.ones((1, blk), F32)) * hb)
        pv = [values(h, jnp.maximum(2 * npair - 1, 0), pb_ref[h]) for h in heads]
        kd = k_ref[i]
        own = [_dot(kd[:, lanes(h)], jnp.concatenate([q64[h], zpad], axis=0)) for h in heads]
        sm = [softmax(jnp.where(causal, own[h], NEG_INF), c[2 * h]) for h in heads]
        for h in heads:
            res = (acc_ref[h] * c[2 * h + 1] + pv[h]) * sm[h][1] + values(h, i, sm[h][2])
            o_ref[h, i] = (res[:ATT_HEAD_DIM] / res[ATT_HEAD_DIM:ATT_HEAD_DIM + 1]).astype(BF16)
        return carry

    lax.fori_loop(0, nb, qblock, 0)


def _moba(q4, v4, k3, km3, batch, hb=MOBA_HEADS_PER_STEP):
    H, nblk_total, dh, blk = q4.shape
    nb = nblk_total // batch
    return pl.pallas_call(
        functools.partial(_moba_kernel, nb=nb, hb=hb),
        grid=(batch, H // hb),
        in_specs=[
            pl.BlockSpec((hb, nb, dh, blk), lambda b, h: (h, b, 0, 0)),
            pl.BlockSpec((hb, nb, dh, blk), lambda b, h: (h, b, 0, 0)),
            pl.BlockSpec((nb, blk, hb * LANES), lambda b, h: (b, 0, h)),
            pl.BlockSpec((1, nb, hb * LANES), lambda b, h: (b, 0, h)),
        ],
        out_specs=pl.BlockSpec((hb, nb, dh, blk), lambda b, h: (h, b, 0, 0)),
        out_shape=jax.ShapeDtypeStruct(q4.shape, BF16),
        scratch_shapes=[pltpu.VMEM((hb, nb, blk), F32),
                        pltpu.VMEM((hb, dh + 16, blk), F32),
                        pltpu.VMEM((hb, blk, blk), F32),
                        pltpu.VMEM((hb, blk, blk), F32),
                        pltpu.VMEM((hb, blk, blk), BF16)],
        compiler_params=_cparams(("arbitrary", "arbitrary")),
        name="moba",
    )(q4, v4, k3, km3)


def _ssm_kernel(u_ref, ar_ref, ai_ref, ldt_ref, arc_ref, aic_ref, btr_ref, bti_ref, cr_ref, ci_ref,
                ctr_ref, cti_ref, d_ref, y_ref,
                u2_ref, m_ref, pr_ref, pi_ref, qr_ref, qi_ref, kv_ref, sr_ref, si_ref, xr_ref, xi_ref,
                *, nbatch):
    L = SSM_CHUNK
    C = SSM_GROUP
    N = u_ref.shape[1]
    nchunk = N // nbatch
    hi = lax.Precision.HIGHEST

    dt = jnp.exp(ldt_ref[0])
    are = jnp.minimum(ar_ref[0], -1e-4)
    aim = ai_ref[0]
    ea, th = are * dt, aim * dt
    mag = jnp.exp(ea)
    lbr, lbi = mag * jnp.cos(th), mag * jnp.sin(th)
    den = are * are + aim * aim
    cfr = ((lbr - 1.0) * are + lbi * aim) / den
    cfi = (lbi * are - (lbr - 1.0) * aim) / den
    btr, bti = btr_ref[0], bti_ref[0]
    bbr = cfr * btr - cfi * bti
    bbi = cfr * bti + cfi * btr
    crr, cri = cr_ref[0], ci_ref[0]

    eac = jnp.minimum(arc_ref[0], -1e-4) * dt
    thc = aic_ref[0] * dt
    tau = lax.broadcasted_iota(I32, (1, L), 1).astype(F32)
    m0 = jnp.exp(eac * tau)
    pw0r, pw0i = m0 * jnp.cos(thc * tau), m0 * jnp.sin(thc * tau)
    m1 = jnp.exp(eac * (tau + 1.0))
    pw1r, pw1i = m1 * jnp.cos(thc * (tau + 1.0)), m1 * jnp.sin(thc * (tau + 1.0))
    back = (L - 1.0) - lax.broadcasted_iota(I32, (L, 1), 0).astype(F32)
    mb = jnp.exp(ea * back)
    pbr, pbi = mb * jnp.cos(th * back), mb * jnp.sin(th * back)
    ml = jnp.exp(ea * float(L))
    alr, ali = ml * jnp.cos(th * float(L)), ml * jnp.sin(th * float(L))

    upper = lax.broadcasted_iota(I32, (L, L), 1) >= lax.broadcasted_iota(I32, (L, L), 0)
    for ci in range(C):
        cbr = bbr[ci:ci + 1, :] * crr - bbi[ci:ci + 1, :] * cri
        cbi = bbr[ci:ci + 1, :] * cri + bbi[ci:ci + 1, :] * crr
        kv_ref[ci * C:(ci + 1) * C, :] = (jnp.dot(cbr, pw0r, precision=hi, preferred_element_type=F32)
                                          - jnp.dot(cbi, pw0i, precision=hi, preferred_element_type=F32))
        pr_ref[ci * L:(ci + 1) * L, :] = (pbr * bbr[ci:ci + 1, :] - pbi * bbi[ci:ci + 1, :]).astype(BF16)
        pi_ref[ci * L:(ci + 1) * L, :] = (pbr * bbi[ci:ci + 1, :] + pbi * bbr[ci:ci + 1, :]).astype(BF16)
        u2_ref[:, ci * L:(ci + 1) * L] = u_ref[ci]
    for co in range(C):
        cc_r, cc_i = ctr_ref[0, co], cti_ref[0, co]
        qr_ref[:, co * L:(co + 1) * L] = (cc_r * pw1r - cc_i * pw1i).astype(BF16)
        qi_ref[:, co * L:(co + 1) * L] = (-(cc_r * pw1i + cc_i * pw1r)).astype(BF16)

    def toeplitz(ci, carry):
        for co in range(C):
            taps = kv_ref[pl.ds(ci * C + co, 1), :]
            t = pltpu.roll(jnp.broadcast_to(taps, (L, L)), 0, 1, stride=1, stride_axis=0)
            m_ref[pl.ds(pl.multiple_of(ci * L, L), L), co * L:(co + 1) * L] = jnp.where(upper, t, 0.0).astype(BF16)
        return carry

    lax.fori_loop(0, C, toeplitz, 0)

    u2 = u2_ref[...]
    sr_ref[...] = _dot(u2, pr_ref[...]).reshape(nbatch, nchunk, LANES)
    si_ref[...] = _dot(u2, pi_ref[...]).reshape(nbatch, nchunk, LANES)

    def chunk_scan(c, carry):
        xr, xi = carry
        xr_ref[:, pl.ds(c, 1), :] = xr
        xi_ref[:, pl.ds(c, 1), :] = xi
        nr = alr * xr - ali * xi + sr_ref[:, pl.ds(c, 1), :]
        ni = alr * xi + ali * xr + si_ref[:, pl.ds(c, 1), :]
        return nr, ni

    zero = jnp.zeros((nbatch, 1, LANES), F32)
    lax.fori_loop(0, nchunk, chunk_scan, (zero, zero))
    xpr = xr_ref[...].reshape(N, LANES).astype(BF16)
    xpi = xi_ref[...].reshape(N, LANES).astype(BF16)

    wide = 4 * L
    for n0 in range(C * L // wide):
        y = (_dot(u2, m_ref[:, n0 * wide:(n0 + 1) * wide])
             + _dot(xpr, qr_ref[:, n0 * wide:(n0 + 1) * wide])
             + _dot(xpi, qi_ref[:, n0 * wide:(n0 + 1) * wide]))
        for cc in range(wide // L):
            co = n0 * (wide // L) + cc
            y_ref[co] = (y[:, cc * L:(cc + 1) * L] + d_ref[0, co] * u_ref[co].astype(F32)).astype(BF16)


def _ssm(zt3, row0, params, nbatch):
    (ar, ai, ldt, arc, aic, btr, bti, cr, ci, ctr, cti, d) = params
    G = ar.shape[0]
    C, L = SSM_GROUP, SSM_CHUNK
    N = zt3.shape[1]
    g3 = lambda g: (g, 0, 0)
    g4 = lambda g: (g, 0, 0, 0)
    return pl.pallas_call(
        functools.partial(_ssm_kernel, nbatch=nbatch),
        grid=(G,),
        in_specs=[
            pl.BlockSpec((C, N, L), lambda g: (row0 // C + g, 0, 0)),
            pl.BlockSpec((1, 1, LANES), g3), pl.BlockSpec((1, 1, LANES), g3), pl.BlockSpec((1, 1, 1), g3),
            pl.BlockSpec((1, LANES, 1), g3), pl.BlockSpec((1, LANES, 1), g3),
            pl.BlockSpec((1, C, LANES), g3), pl.BlockSpec((1, C, LANES), g3),
            pl.BlockSpec((1, C, LANES), g3), pl.BlockSpec((1, C, LANES), g3),
            pl.BlockSpec((1, C, LANES, 1), g4), pl.BlockSpec((1, C, LANES, 1), g4),
            pl.BlockSpec((1, C, 1, 1), g4),
        ],
        out_specs=pl.BlockSpec((C, N, L), g3),
        out_shape=jax.ShapeDtypeStruct((G * C, N, L), BF16),
        scratch_shapes=[
            pltpu.VMEM((N, C * L), BF16),
            pltpu.VMEM((C * L, C * L), BF16),
            pltpu.VMEM((C * L, LANES), BF16), pltpu.VMEM((C * L, LANES), BF16),
            pltpu.VMEM((LANES, C * L), BF16), pltpu.VMEM((LANES, C * L), BF16),
            pltpu.VMEM((C * C, L), F32),
            pltpu.VMEM((nbatch, N // nbatch, LANES), F32), pltpu.VMEM((nbatch, N // nbatch, LANES), F32),
            pltpu.VMEM((nbatch, N // nbatch, LANES), F32), pltpu.VMEM((nbatch, N // nbatch, LANES), F32),
        ],
        compiler_params=_cparams(("arbitrary",)),
        name="ssm",
    )(zt3, ar, ai, ldt, arc, aic, btr, bti, cr, ci, ctr, cti, d)


def _mixer_out_kernel(ga_ref, gb_ref, ys_ref, at_ref, x_ref, wglu_ref, wpa_ref, wout_ref, h_ref):
    dm = ga_ref.shape[0]
    nblk = at_ref.shape[1]
    att = jnp.concatenate(
        [jnp.concatenate([at_ref[h, j] for j in range(nblk)], axis=1) for h in range(ATT_HEADS)], axis=0)
    ya = _dot(wpa_ref[...], att)
    gl = jax.nn.gelu(ys_ref[...].astype(F32)).astype(BF16)
    zb = _dot(wglu_ref[...], gl)
    yb = zb[:dm] * jax.nn.sigmoid(zb[dm:])
    merged = jax.nn.sigmoid(ga_ref[...].astype(F32)) * ya + jax.nn.sigmoid(gb_ref[...].astype(F32)) * yb
    h_ref[...] = x_ref[...] + _dot_tn(merged.astype(BF16), wout_ref[...])


def _mixer_out(zt, ys_t, att4, x2, wglu_t, wpa_t, wout, tm=512):
    T, D = x2.shape
    nblk = tm // MOBA_BLOCK
    sw = ys_t.shape[0]
    full = lambda a: pl.BlockSpec(a.shape, lambda i: (0,) * a.ndim)
    return pl.pallas_call(
        _mixer_out_kernel,
        grid=(T // tm,),
        in_specs=[
            pl.BlockSpec((D, tm), lambda i: (0, i)),
            pl.BlockSpec((D, tm), lambda i: (1, i)),
            pl.BlockSpec((sw, tm), lambda i: (0, i)),
            pl.BlockSpec((ATT_HEADS, nblk, ATT_HEAD_DIM, MOBA_BLOCK), lambda i: (0, i, 0, 0)),
            pl.BlockSpec((tm, D), lambda i: (i, 0)),
            full(wglu_t), full(wpa_t), full(wout),
        ],
        out_specs=pl.BlockSpec((tm, D), lambda i: (i, 0)),
        out_shape=jax.ShapeDtypeStruct((T, D), F32),
        compiler_params=_cparams(("arbitrary",)),
        name="mixer_out",
    )(zt, zt, ys_t, att4, x2, wglu_t, wpa_t, wout)


def _memkv_kernel(mem_ref, g_ref, w_ref, kv_ref):
    kv_ref[...] = _dot(_rmsnorm(mem_ref[...], g_ref[...]).astype(BF16), w_ref[...]).astype(BF16)


def _memkv(mem2, g, w, rows):
    M, D = mem2.shape
    return pl.pallas_call(
        _memkv_kernel,
        grid=(M // rows,),
        in_specs=[pl.BlockSpec((rows, D), lambda i: (i, 0)), pl.BlockSpec((1, D), lambda i: (0, 0)),
                  pl.BlockSpec(w.shape, lambda i: (0, 0))],
        out_specs=pl.BlockSpec((rows, w.shape[1]), lambda i: (i, 0)),
        out_shape=jax.ShapeDtypeStruct((M, w.shape[1]), BF16),
        compiler_params=_cparams(("arbitrary",)),
        name="memkv",
    )(mem2, g, w)


def _xattn_kernel(h_ref, g_ref, wq_ref, kv_ref, wo_ref, o_ref):
    h = h_ref[...]
    q = _dot(_rmsnorm(h, g_ref[...]).astype(BF16), wq_ref[...]).astype(BF16)
    kv = kv_ref[...]
    xw = XATTN_HEADS * XATTN_HEAD_DIM
    scale = XATTN_HEAD_DIM ** -0.5
    outs = []
    for hd in range(XATTN_HEADS):
        sl = slice(hd * XATTN_HEAD_DIM, (hd + 1) * XATTN_HEAD_DIM)
        s = _dot_nt(q[:, sl], kv[:, sl]) * scale
        p = jnp.exp(s - jnp.max(s, axis=-1, keepdims=True))
        p = p / jnp.sum(p, axis=-1, keepdims=True)
        outs.append(_dot(p.astype(BF16), kv[:, xw + hd * XATTN_HEAD_DIM:xw + (hd + 1) * XATTN_HEAD_DIM]))
    o = jnp.concatenate(outs, axis=1).astype(BF16)
    o_ref[...] = h + _dot(o, wo_ref[...])


def _xattn(h1, g, wq, kv, wo, seq, mem_len, tm=512):
    T, D = h1.shape
    per_b = seq // tm
    return pl.pallas_call(
        _xattn_kernel,
        grid=(T // tm,),
        in_specs=[
            pl.BlockSpec((tm, D), lambda i: (i, 0)),
            pl.BlockSpec((1, D), lambda i: (0, 0)),
            pl.BlockSpec(wq.shape, lambda i: (0, 0)),
            pl.BlockSpec((mem_len, kv.shape[1]), lambda i: (i // per_b, 0)),
            pl.BlockSpec(wo.shape, lambda i: (0, 0)),
        ],
        out_specs=pl.BlockSpec((tm, D), lambda i: (i, 0)),
        out_shape=jax.ShapeDtypeStruct((T, D), F32),
        compiler_params=_cparams(("arbitrary",)),
        name="xattn",
    )(h1, g, wq, kv, wo)


def _router_kernel(h_ref, g_ref, wr_ref, br_ref, xn_ref, e_ref, r_ref, gate_ref, cnt_ref, carry_ref):
    i = pl.program_id(0)
    E = N_EXPERTS
    tm = h_ref.shape[0]

    @pl.when(i == 0)
    def _():
        carry_ref[...] = jnp.zeros_like(carry_ref)

    n = _rmsnorm(h_ref[...], g_ref[...])
    _to_row_tiles(xn_ref, n)
    na, nb_, nc = _split3(n)
    wa, wb, wc = _split3(wr_ref[...])
    logits = (_dot_nt(wa, na) + (_dot_nt(wa, nb_) + _dot_nt(wb, na))
              + (_dot_nt(wb, nb_) + _dot_nt(wa, nc) + _dot_nt(wc, na))) + br_ref[...]
    eidx = lax.broadcasted_iota(I32, (E, tm), 0)
    g = logits
    picks, vals = [], []
    for _ in range(EXPERT_TOPK):
        mx = jnp.max(g, axis=0, keepdims=True)
        first = jnp.min(jnp.where(g == mx, eidx, E), axis=0, keepdims=True)
        pick = eidx == first
        picks.append(pick)
        vals.append(mx)
        g = jnp.where(pick, -3e38, g)
    ex = [jnp.exp(v - vals[0]) for v in vals]
    tot = ex[0] + ex[1] + ex[2] + ex[3]
    sel = jnp.zeros((E, tm), F32)
    for pk in picks:
        sel = jnp.where(pk, 1.0, sel)
    before = (lax.broadcasted_iota(I32, (tm, tm), 0) < lax.broadcasted_iota(I32, (tm, tm), 1))
    prefix = _dot(sel.astype(BF16), jnp.where(before, 1.0, 0.0).astype(BF16))
    pos = prefix + carry_ref[:, 0:1]
    zi = jnp.zeros((8 - EXPERT_TOPK, tm), I32)
    e_rows = [jnp.sum(jnp.where(pk, eidx, 0), axis=0, keepdims=True) for pk in picks]
    r_rows = [jnp.sum(jnp.where(pk, pos, 0.0), axis=0, keepdims=True).astype(I32) for pk in picks]
    e_ref[...] = jnp.concatenate(e_rows + [zi], axis=0)
    r_ref[...] = jnp.concatenate(r_rows + [zi], axis=0)
    gate_rows = jnp.concatenate([x / tot for x in ex] + [jnp.zeros((LANES - EXPERT_TOPK, tm), F32)], axis=0)
    gate_ref[...] = gate_rows.T
    carry_ref[...] = carry_ref[...] + jnp.sum(sel, axis=1, keepdims=True)
    cnt_ref[...] = carry_ref[...]


def _router(h2, g, wr_t, br, tm=256):
    T, D = h2.shape
    E = N_EXPERTS
    return pl.pallas_call(
        _router_kernel,
        grid=(T // tm,),
        in_specs=[pl.BlockSpec((tm, D), lambda i: (i, 0)), pl.BlockSpec((1, D), lambda i: (0, 0)),
                  pl.BlockSpec((E, D), lambda i: (0, 0)), pl.BlockSpec((E, 1), lambda i: (0, 0))],
        out_specs=[pl.BlockSpec((tm * ROW_TILE, LANES), lambda i: (i, 0)),
                   pl.BlockSpec((8, tm), lambda i: (0, i)), pl.BlockSpec((8, tm), lambda i: (0, i)),
                   pl.BlockSpec((tm, LANES), lambda i: (i, 0)),
                   pl.BlockSpec((E, LANES), lambda i: (0, 0))],
        out_shape=[jax.ShapeDtypeStruct((T * ROW_TILE, LANES), F32),
                   jax.ShapeDtypeStruct((8, T), I32), jax.ShapeDtypeStruct((8, T), I32),
                   jax.ShapeDtypeStruct((T, LANES), F32),
                   jax.ShapeDtypeStruct((E, LANES), F32)],
        scratch_shapes=[pltpu.VMEM((E, LANES), F32)],
        compiler_params=_cparams(("arbitrary",)),
        name="router",
    )(h2, g, wr_t, br)


def _positions_kernel(ps_ref, e_ref, r_ref, pos_ref):
    e = e_ref[...]
    pos = r_ref[...]
    for x in range(N_EXPERTS):
        pos = pos + jnp.where(e == x, ps_ref[x], 0)
    for c in range(pos_ref.shape[0]):
        pos_ref[c] = pos[:, c * LANES:(c + 1) * LANES]


def _positions(pstart, e8, r8, tm=2048):
    T = e8.shape[1]
    grid_spec = pltpu.PrefetchScalarGridSpec(
        num_scalar_prefetch=1,
        grid=(T // tm,),
        in_specs=[pl.BlockSpec((8, tm), lambda i, ps: (0, i)), pl.BlockSpec((8, tm), lambda i, ps: (0, i))],
        out_specs=pl.BlockSpec((tm // LANES, 8, LANES), lambda i, ps: (i, 0, 0)),
    )
    return pl.pallas_call(
        _positions_kernel, grid_spec=grid_spec, out_shape=jax.ShapeDtypeStruct((T // LANES, 8, LANES), I32),
        compiler_params=_cparams(("arbitrary",)), name="positions",
    )(pstart, e8, r8)


def _row_dmas(tm, make_copy, wait):
    for k in range(EXPERT_TOPK):
        for c in range(tm // LANES):
            def body(lane, carry, k=k, c=c):
                cp = make_copy(c, k, lane)
                cp.wait() if wait else cp.start(priority=k % 2)
                return carry
            lax.fori_loop(0, LANES, body, 0, unroll=8)


def _dispatch_kernel(zr_ref, pos_ref, xn_ref, xs_ref, zbuf, sem, zsem):
    tm = xn_ref.shape[0] // ROW_TILE

    @pl.when(pl.program_id(0) == 0)
    def _():
        zbuf[...] = jnp.zeros_like(zbuf)

        def zero_copy(e):
            first = pl.multiple_of(zr_ref[e] * ROW_TILE, MOE_BLOCK * ROW_TILE)
            return pltpu.make_async_copy(zbuf, xs_ref.at[pl.ds(first, MOE_BLOCK * ROW_TILE)], zsem)

        for e in range(2 * N_EXPERTS):
            @pl.when(zr_ref[e] >= 0)
            def _():
                zero_copy(e).start()
        for e in range(2 * N_EXPERTS):
            @pl.when(zr_ref[e] >= 0)
            def _():
                zero_copy(e).wait()

    row_copy = lambda c, k, lane: pltpu.make_async_copy(
        _tile(xn_ref, c * LANES + lane), _tile(xs_ref, pos_ref[c, k, lane]), sem)
    _row_dmas(tm, row_copy, wait=False)
    _row_dmas(tm, row_copy, wait=True)


def _dispatch(zrow, pos3, xn, rows_total, tm=1024):
    T = xn.shape[0] // ROW_TILE
    grid_spec = pltpu.PrefetchScalarGridSpec(
        num_scalar_prefetch=1,
        grid=(T // tm,),
        in_specs=[pl.BlockSpec((tm // LANES, 8, LANES), lambda i, zr: (i, 0, 0), memory_space=pltpu.SMEM),
                  pl.BlockSpec((tm * ROW_TILE, LANES), lambda i, zr: (i, 0))],
        out_specs=pl.BlockSpec(memory_space=pl.ANY),
        scratch_shapes=[pltpu.VMEM((MOE_BLOCK * ROW_TILE, LANES), F32), pltpu.SemaphoreType.DMA(()),
                        pltpu.SemaphoreType.DMA(())],
    )
    return pl.pallas_call(
        _dispatch_kernel,
        grid_spec=grid_spec,
        out_shape=jax.ShapeDtypeStruct((rows_total * ROW_TILE, LANES), F32),
        compiler_params=_cparams(("arbitrary",)),
        name="dispatch",
    )(zrow, pos3, xn)


def _experts_kernel(be_ref, nu_ref, xs_ref, wgu_ref, bgu_ref, wdn_ref, bdn_ref, ys_ref, wgu_bf, wdn_bf):
    i = pl.program_id(0)
    dff = wdn_ref.shape[1]

    @pl.when(i < nu_ref[0])
    def _():
        prev = be_ref[jnp.maximum(i - 1, 0)]

        @pl.when((i == 0) | (be_ref[i] != prev))
        def _():
            wgu_bf[...] = wgu_ref[0].astype(BF16)
            wdn_bf[...] = wdn_ref[0].astype(BF16)

        gu = _dot(_from_row_tiles(xs_ref, MOE_BLOCK).astype(BF16), wgu_bf[...]) + bgu_ref[0]
        gate = jnp.minimum(gu[:, :dff], SWIGLU_LIMIT)
        up = jnp.clip(gu[:, dff:], -SWIGLU_LIMIT, SWIGLU_LIMIT)
        hid = (up + 1.0) * gate * jax.nn.sigmoid(SWIGLU_ALPHA * gate)
        _to_row_tiles(ys_ref, _dot(hid.astype(BF16), wdn_bf[...]) + bdn_ref[0])

    @pl.when(i >= nu_ref[0])
    def _():
        ys_ref[...] = jnp.zeros_like(ys_ref)


def _experts(blk_expert, nused, xs, wgu, bgu, wdn, bdn):
    rows = xs.shape[0] // ROW_TILE
    E, D, F2 = wgu.shape
    dff = wdn.shape[1]
    nblk = rows // MOE_BLOCK
    grid_spec = pltpu.PrefetchScalarGridSpec(
        num_scalar_prefetch=2,
        grid=(nblk,),
        in_specs=[pl.BlockSpec((MOE_BLOCK * ROW_TILE, LANES), lambda i, be, nu: (jnp.minimum(i, nu[0] - 1), 0)),
                  pl.BlockSpec((1, D, F2), lambda i, be, nu: (be[i], 0, 0)),
                  pl.BlockSpec((1, 1, F2), lambda i, be, nu: (be[i], 0, 0)),
                  pl.BlockSpec((1, dff, D), lambda i, be, nu: (be[i], 0, 0)),
                  pl.BlockSpec((1, 1, D), lambda i, be, nu: (be[i], 0, 0))],
        out_specs=pl.BlockSpec((MOE_BLOCK * ROW_TILE, LANES), lambda i, be, nu: (i, 0)),
        scratch_shapes=[pltpu.VMEM((D, F2), BF16), pltpu.VMEM((dff, D), BF16)],
    )
    return pl.pallas_call(
        _experts_kernel,
        grid_spec=grid_spec,
        out_shape=jax.ShapeDtypeStruct((rows * ROW_TILE, LANES), F32),
        compiler_params=_cparams(("arbitrary",)),
        name="experts",
    )(blk_expert, nused, xs, wgu, bgu, wdn, bdn)


def _combine_kernel(pos_ref, nxt_ref, h_ref, gate_ref, gf_ref, ys_ref, o_ref, buf, sem):
    i = pl.program_id(0)
    tm = h_ref.shape[0]
    slot = i % 2

    def gather(p_ref, slot):
        first = slot * (EXPERT_TOPK * tm)
        return lambda c, k, lane: pltpu.make_async_copy(
            _tile(ys_ref, p_ref[c, k, lane]), _tile(buf, first + k * tm + c * LANES + lane), sem.at[slot])

    @pl.when(i == 0)
    def _():
        _row_dmas(tm, gather(pos_ref, slot), wait=False)

    @pl.when(i + 1 < pl.num_programs(0))
    def _():
        _row_dmas(tm, gather(nxt_ref, 1 - slot), wait=False)

    _row_dmas(tm, gather(pos_ref, slot), wait=True)
    gates = gate_ref[...]
    acc = h_ref[...]
    for k in range(EXPERT_TOPK):
        acc = acc + gates[:, k:k + 1] * _from_row_tiles(buf, tm, first=slot * (EXPERT_TOPK * tm) + k * tm)
    o_ref[...] = _rmsnorm(acc, gf_ref[...])


def _combine(pos3, h2, gates, gf, ys, tm=256):
    T, D = h2.shape
    return pl.pallas_call(
        _combine_kernel,
        grid=(T // tm,),
        in_specs=[pl.BlockSpec((tm // LANES, 8, LANES), lambda i: (i, 0, 0), memory_space=pltpu.SMEM),
                  pl.BlockSpec((tm // LANES, 8, LANES), lambda i: (jnp.minimum(i + 1, T // tm - 1), 0, 0),
                               memory_space=pltpu.SMEM),
                  pl.BlockSpec((tm, D), lambda i: (i, 0)),
                  pl.BlockSpec((tm, LANES), lambda i: (i, 0)),
                  pl.BlockSpec((1, D), lambda i: (0, 0)),
                  pl.BlockSpec(memory_space=pl.ANY)],
        out_specs=pl.BlockSpec((tm, D), lambda i: (i, 0)),
        out_shape=jax.ShapeDtypeStruct((T, D), F32),
        scratch_shapes=[pltpu.VMEM((2 * EXPERT_TOPK * tm * ROW_TILE, LANES), F32), pltpu.SemaphoreType.DMA((2,))],
        compiler_params=_cparams(("arbitrary",)),
        name="combine",
    )(pos3, pos3, h2, gates, gf, ys)


def _pad_lanes(a, width=LANES, value=0.0):
    pad = [(0, 0)] * (a.ndim - 1) + [(0, width - a.shape[-1])]
    return jnp.pad(a, pad, constant_values=value)


def _layer(h2d, mem2d, batch, seq, mem_len, g_mix, w_in, a_re, a_im, log_dt, b_re, b_im, c_re, c_im, d_skip,
           w_glu, w_proj_a, w_out, g_xattn, g_mem, w_xq, w_xkv, w_xo, g_moe, w_router, b_router,
           w_gu, b_gu, w_dn, b_dn, g_final):
    T, D = h2d.shape
    aw = ATT_HEADS * ATT_HEAD_DIM
    G = a_re.shape[0]
    sw = G * SSM_GROUP

    wq, wk, wv, wu, wga, wgb = jnp.split(w_in, [aw, 2 * aw, 3 * aw, 3 * aw + sw, 3 * aw + sw + D], axis=1)
    wg_t = jnp.concatenate([wga, wgb, wu], axis=1).T.astype(BF16)
    wqv_t = jnp.concatenate([wq * (ATT_HEAD_DIM ** -0.5), wv], axis=1).T.astype(BF16)
    wk_pad = _pad_lanes(wk.reshape(D, ATT_HEADS, ATT_HEAD_DIM)).reshape(D, ATT_HEADS * LANES).astype(BF16)

    zt, q4, v4, kpad, km = _in_proj(h2d, g_mix.reshape(1, D), wg_t, wqv_t, wk_pad)
    nb = seq // MOBA_BLOCK
    att4 = _moba(q4, v4, kpad.reshape(T // MOBA_BLOCK, MOBA_BLOCK, ATT_HEADS * LANES),
                 km.reshape(batch, nb, ATT_HEADS * LANES), batch)

    row = lambda a, v=0.0: _pad_lanes(a, value=v).reshape(G, 1, LANES)
    col = lambda a, v=0.0: _pad_lanes(a, value=v).reshape(G, LANES, 1)
    ssm_params = (
        row(a_re, -1.0), row(a_im), log_dt.reshape(G, 1, 1), col(a_re, -1.0), col(a_im),
        _pad_lanes(jnp.swapaxes(b_re, 1, 2)), _pad_lanes(jnp.swapaxes(b_im, 1, 2)),
        _pad_lanes(c_re), _pad_lanes(c_im),
        _pad_lanes(c_re)[..., None], _pad_lanes(c_im)[..., None],
        d_skip.reshape(G, SSM_GROUP, 1, 1),
    )
    ys3 = _ssm(zt.reshape(zt.shape[0], T // SSM_CHUNK, SSM_CHUNK), 2 * D, ssm_params, batch)
    h1 = _mixer_out(zt, ys3.reshape(sw, T), att4, h2d, w_glu.T.astype(BF16), w_proj_a.T.astype(BF16),
                    w_out.astype(BF16))

    kv = _memkv(mem2d, g_mem.reshape(1, D), w_xkv.astype(BF16), mem_len)
    h2 = _xattn(h1, g_xattn.reshape(1, D), w_xq.astype(BF16), kv, w_xo.astype(BF16), seq, mem_len)

    xn, e8, r8, gates, cnt = _router(h2, g_moe.reshape(1, D), w_router.T, b_router.reshape(N_EXPERTS, 1))
    counts = cnt[:, 0].astype(I32)
    nblk_e = (counts + MOE_BLOCK - 1) // MOE_BLOCK
    bends = jnp.cumsum(nblk_e)
    pstart = ((bends - nblk_e) * MOE_BLOCK).astype(I32)
    nblk = (T * EXPERT_TOPK) // MOE_BLOCK + N_EXPERTS
    blk_expert = jnp.minimum(jnp.sum(bends[None, :] <= jnp.arange(nblk, dtype=I32)[:, None], axis=1),
                             N_EXPERTS - 1).astype(I32)
    nused = bends[-1:].astype(I32)
    blk_expert = jnp.where(jnp.arange(nblk) < nused[0], blk_expert, blk_expert[jnp.maximum(nused[0] - 1, 0)])

    tail = nblk - 1 - jnp.arange(N_EXPERTS, dtype=I32)
    zrow = jnp.concatenate([jnp.where(nblk_e > 0, (bends - 1) * MOE_BLOCK, -1),
                            jnp.where(tail >= nused[0], tail * MOE_BLOCK, -1)]).astype(I32)
    pos3 = _positions(pstart, e8, r8)
    xs = _dispatch(zrow, pos3, xn, nblk * MOE_BLOCK)
    ys = _experts(blk_expert, nused, xs, w_gu, b_gu.reshape(N_EXPERTS, 1, -1), w_dn, b_dn.reshape(N_EXPERTS, 1, -1))
    return _combine(pos3, h2, gates, g_final.reshape(1, D), ys)


def kernel(x, mem, g_mix, w_in, a_re, a_im, log_dt, b_re, b_im, c_re, c_im, d_skip, w_glu, w_proj_a, w_out,
           g_xattn, g_mem, w_xq, w_xkv, w_xo, g_moe, w_router, b_router, w_gu, b_gu, w_dn, b_dn, g_final):
    B, S, D = x.shape
    M = mem.shape[1]
    assert g_mix.shape[0] == 1, "single layer"
    out = _layer(x.reshape(B * S, D), mem.reshape(B * M, D), B, S, M,
                 g_mix[0], w_in[0], a_re[0], a_im[0], log_dt[0], b_re[0], b_im[0], c_re[0], c_im[0], d_skip[0],
                 w_glu[0], w_proj_a[0], w_out[0], g_xattn[0], g_mem[0], w_xq[0], w_xkv[0], w_xo[0],
                 g_moe[0], w_router[0], b_router[0], w_gu[0], b_gu[0], w_dn[0], b_dn[0], g_final)
    return out.reshape(B, S, D)
```

```python
import functools
import math

import jax
import jax.numpy as jnp
from jax import lax
from jax.experimental import pallas as pl
from jax.experimental.pallas import tpu as pltpu

F32 = jnp.float32
BF16 = jnp.bfloat16
I32 = jnp.int32

RMS_EPS = 1e-6
NEG_INF = -1e30
ATT_HEADS = 8
ATT_HEAD_DIM = 64
MOBA_BLOCK = 256
MOBA_TOPK = 3
MOBA_HEADS_PER_STEP = 4
SSM_GROUP = 16
SSM_STATE = 64
SSM_CHUNK = 128
XATTN_HEADS = 4
XATTN_HEAD_DIM = 128
N_EXPERTS = 32
EXPERT_TOPK = 4
SWIGLU_LIMIT = 7.0
SWIGLU_ALPHA = 1.702
MOE_BLOCK = 512
LANES = 128
VMEM_LIMIT = 56 * 1024 * 1024


def _cparams(sem):
    return pltpu.CompilerParams(dimension_semantics=sem, vmem_limit_bytes=VMEM_LIMIT)


def _rmsnorm(x, g):
    return x * lax.rsqrt(jnp.mean(x * x, axis=-1, keepdims=True) + RMS_EPS) * g


def _dot(a, b):
    return jnp.dot(a, b, preferred_element_type=F32)


def _dot_nt(a, b):
    return lax.dot_general(a, b, (((1,), (1,)), ((), ())), preferred_element_type=F32)


def _dot_tn(a, b):
    return lax.dot_general(a, b, (((0,), (0,)), ((), ())), preferred_element_type=F32)


def _split3(x):
    a = x.astype(BF16)
    r = x - a.astype(F32)
    b = r.astype(BF16)
    c = (r - b.astype(F32)).astype(BF16)
    return a, b, c


ROW_TILE = 8


def _to_row_tiles(ref, x, first=0):
    for s in range(ROW_TILE):
        ref[pl.ds(first * ROW_TILE + s, x.shape[0], stride=ROW_TILE), :] = x[:, s * LANES:(s + 1) * LANES]


def _from_row_tiles(ref, rows, first=0):
    return jnp.concatenate(
        [ref[pl.ds(first * ROW_TILE + s, rows, stride=ROW_TILE), :] for s in range(ROW_TILE)], axis=1)


def _pos(pos_ref, c, k, lane):
    return pos_ref[(c * 8 + k) * LANES + lane]


def _tile(ref, r):
    return ref.at[pl.ds(pl.multiple_of(r * ROW_TILE, ROW_TILE), ROW_TILE)]


def _inproj_kernel(x_ref, g_ref, wg_ref, wqv_ref, wk_ref, zt_ref, q_ref, v_ref, k_ref, km_ref):
    n = _rmsnorm(x_ref[...], g_ref[...]).astype(BF16)
    tm = n.shape[0]
    nblk = tm // MOBA_BLOCK
    rows = 512
    for c in range(wg_ref.shape[0] // rows):
        zt_ref[c * rows:(c + 1) * rows, :] = _dot_nt(wg_ref[c * rows:(c + 1) * rows, :], n).astype(BF16)
    for c, dst in ((0, q_ref), (1, v_ref)):
        r = _dot_nt(wqv_ref[c * rows:(c + 1) * rows, :], n).astype(BF16)
        for h in range(ATT_HEADS):
            for j in range(nblk):
                dst[h, j] = r[h * ATT_HEAD_DIM:(h + 1) * ATT_HEAD_DIM, j * MOBA_BLOCK:(j + 1) * MOBA_BLOCK]
    k = _dot(n, wk_ref[...])
    lane = lax.broadcasted_iota(I32, (1, k.shape[1]), 1) & (LANES - 1)
    k = k + jnp.where(lane == ATT_HEAD_DIM, 1.0, 0.0)
    k_ref[...] = k.astype(BF16)
    km_ref[0] = jnp.mean(k.reshape(nblk, MOBA_BLOCK, k.shape[1]), axis=1)


def _in_proj(x2, g, wg_t, wqv_t, wk_pad, tm=512):
    T, D = x2.shape
    nblk = tm // MOBA_BLOCK
    ng = wg_t.shape[0]
    kw = wk_pad.shape[1]
    hd = (ATT_HEADS, T // MOBA_BLOCK, ATT_HEAD_DIM, MOBA_BLOCK)
    return pl.pallas_call(
        _inproj_kernel,
        grid=(T // tm,),
        in_specs=[
            pl.BlockSpec((tm, D), lambda i: (i, 0)),
            pl.BlockSpec((1, D), lambda i: (0, 0)),
            pl.BlockSpec((ng, D), lambda i: (0, 0)),
            pl.BlockSpec(wqv_t.shape, lambda i: (0, 0)),
            pl.BlockSpec((D, kw), lambda i: (0, 0)),
        ],
        out_specs=[
            pl.BlockSpec((ng, tm), lambda i: (0, i)),
            pl.BlockSpec((ATT_HEADS, nblk, ATT_HEAD_DIM, MOBA_BLOCK), lambda i: (0, i, 0, 0)),
            pl.BlockSpec((ATT_HEADS, nblk, ATT_HEAD_DIM, MOBA_BLOCK), lambda i: (0, i, 0, 0)),
            pl.BlockSpec((tm, kw), lambda i: (i, 0)),
            pl.BlockSpec((1, nblk, kw), lambda i: (i, 0, 0)),
        ],
        out_shape=[
            jax.ShapeDtypeStruct((ng, T), BF16),
            jax.ShapeDtypeStruct(hd, BF16),
            jax.ShapeDtypeStruct(hd, BF16),
            jax.ShapeDtypeStruct((T, kw), BF16),
            jax.ShapeDtypeStruct((T // tm, nblk, kw), F32),
        ],
        compiler_params=_cparams(("arbitrary",)),
        name="in_proj",
    )(x2, g, wg_t, wqv_t, wk_pad)


def _moba_kernel(q_ref, v_ref, k_ref, km_ref, o_ref, sel_ref, acc_ref, sa_ref, sb_ref, pb_ref, *, nb, hb):
    blk = MOBA_BLOCK
    heads = range(hb)
    lanes = lambda h: slice(h * LANES, (h + 1) * LANES)
    kms = [_split3(km_ref[0][:, lanes(h)]) for h in heads]
    zpad = jnp.zeros((LANES - ATT_HEAD_DIM, blk), BF16)
    bidx = lax.broadcasted_iota(I32, (nb, blk), 0)
    causal = lax.broadcasted_iota(I32, (blk, blk), 0) <= lax.broadcasted_iota(I32, (blk, blk), 1)
    topk = min(MOBA_TOPK, nb)

    row16 = lax.broadcasted_iota(I32, (16, blk), 0) == 0
    ones16 = jnp.where(row16, 1.0, 0.0).astype(BF16)
    zpad48 = jnp.zeros((LANES - ATT_HEAD_DIM - 16, blk), BF16)

    def qblock(i, carry):
        elig = bidx < i
        q64 = [q_ref[h, i] for h in heads]
        for h in heads:
            q = jnp.concatenate([q64[h], zpad], axis=0)
            gate = (_dot(kms[h][0], q) + _dot(kms[h][1], q)) + _dot(kms[h][2], q)
            g = jnp.where(elig, gate, NEG_INF)
            sel = jnp.zeros((nb, blk), F32)
            for _ in range(topk):
                mx = jnp.max(g, axis=0, keepdims=True)
                first = jnp.min(jnp.where(g == mx, bidx, nb), axis=0, keepdims=True)
                pick = bidx == first
                sel = jnp.where(pick, 1.0, sel)
                g = jnp.where(pick, -3e38, g)
            sel_ref[h] = jnp.where(elig, sel, 0.0)
            acc_ref[h] = jnp.zeros((ATT_HEAD_DIM + 16, blk), F32)
            pb_ref[h] = jnp.zeros((blk, blk), BF16)

        def scores(h, t):
            bias = (sel_ref[h, pl.ds(t, 1), :] - 1.0) * 1e30
            q = jnp.concatenate([q64[h], jnp.where(row16, bias, 0.0).astype(BF16), zpad48], axis=0)
            return _dot(k_ref[t][:, lanes(h)], q)

        def values(h, t, p):
            return _dot(jnp.concatenate([v_ref[h, t], ones16], axis=0), p)

        def softmax(s, m):
            m_new = jnp.maximum(m, jnp.max(s, axis=0, keepdims=True))
            return m_new, jnp.exp(m - m_new), jnp.exp(s - m_new).astype(BF16)

        for h in heads:
            sa_ref[h] = scores(h, 0)

        def kvpair(u, c):
            ta, tb, tn = 2 * u, 2 * u + 1, jnp.minimum(2 * u + 2, nb - 1)
            pv = [values(h, jnp.maximum(ta - 1, 0), pb_ref[h]) for h in heads]
            sb = [scores(h, tb) for h in heads]
            for h in heads:
                sb_ref[h] = sb[h]
                acc_ref[h] = acc_ref[h] * c[2 * h + 1] + pv[h]
            sm = [softmax(sa_ref[h], c[2 * h]) for h in heads]
            pv = [values(h, ta, sm[h][2]) for h in heads]
            sa = [scores(h, tn) for h in heads]
            out = []
            for h in heads:
                sa_ref[h] = sa[h]
                acc_ref[h] = acc_ref[h] * sm[h][1] + pv[h]
                m_new, alpha, p = softmax(sb_ref[h], sm[h][0])
                pb_ref[h] = p
                out += [m_new, alpha]
            return tuple(out)

        m0 = jnp.full((1, blk), -1e29, F32)
        npair = (i + 1) // 2
        c = lax.fori_loop(0, npair, kvpair, (m0, jnp.ones((1, blk), F32)) * hb)
        pv = [values(h, jnp.maximum(2 * npair - 1, 0), pb_ref[h]) for h in heads]
        kd = k_ref[i]
        own = [_dot(kd[:, lanes(h)], jnp.concatenate([q64[h], zpad], axis=0)) for h in heads]
        sm = [softmax(jnp.where(causal, own[h], NEG_INF), c[2 * h]) for h in heads]
        for h in heads:
            res = (acc_ref[h] * c[2 * h + 1] + pv[h]) * sm[h][1] + values(h, i, sm[h][2])
            o_ref[h, i] = (res[:ATT_HEAD_DIM] / res[ATT_HEAD_DIM:ATT_HEAD_DIM + 1]).astype(BF16)
        return carry

    lax.fori_loop(0, nb, qblock, 0)


def _moba(q4, v4, k3, km3, batch, hb=MOBA_HEADS_PER_STEP):
    H, nblk_total, dh, blk = q4.shape
    nb = nblk_total // batch
    return pl.pallas_call(
        functools.partial(_moba_kernel, nb=nb, hb=hb),
        grid=(batch, H // hb),
        in_specs=[
            pl.BlockSpec((hb, nb, dh, blk), lambda b, h: (h, b, 0, 0)),
            pl.BlockSpec((hb, nb, dh, blk), lambda b, h: (h, b, 0, 0)),
            pl.BlockSpec((nb, blk, hb * LANES), lambda b, h: (b, 0, h)),
            pl.BlockSpec((1, nb, hb * LANES), lambda b, h: (b, 0, h)),
        ],
        out_specs=pl.BlockSpec((hb, nb, dh, blk), lambda b, h: (h, b, 0, 0)),
        out_shape=jax.ShapeDtypeStruct(q4.shape, BF16),
        scratch_shapes=[pltpu.VMEM((hb, nb, blk), F32),
                        pltpu.VMEM((hb, dh + 16, blk), F32),
                        pltpu.VMEM((hb, blk, blk), F32),
                        pltpu.VMEM((hb, blk, blk), F32),
                        pltpu.VMEM((hb, blk, blk), BF16)],
        compiler_params=_cparams(("arbitrary", "arbitrary")),
        name="moba",
    )(q4, v4, k3, km3)


def _ssm_kernel(u_ref, ar_ref, ai_ref, ldt_ref, arc_ref, aic_ref, btr_ref, bti_ref, cr_ref, ci_ref,
                ctr_ref, cti_ref, d_ref, y_ref,
                u2_ref, m_ref, pr_ref, pi_ref, qr_ref, qi_ref, kv_ref, sr_ref, si_ref, xr_ref, xi_ref,
                *, nbatch):
    L = SSM_CHUNK
    C = SSM_GROUP
    N = u_ref.shape[1]
    nchunk = N // nbatch
    hi = lax.Precision.HIGHEST

    dt = jnp.exp(ldt_ref[0])
    are = jnp.minimum(ar_ref[0], -1e-4)
    aim = ai_ref[0]
    ea, th = are * dt, aim * dt
    mag = jnp.exp(ea)
    lbr, lbi = mag * jnp.cos(th), mag * jnp.sin(th)
    den = are * are + aim * aim
    cfr = ((lbr - 1.0) * are + lbi * aim) / den
    cfi = (lbi * are - (lbr - 1.0) * aim) / den
    btr, bti = btr_ref[0], bti_ref[0]
    bbr = cfr * btr - cfi * bti
    bbi = cfr * bti + cfi * btr
    crr, cri = cr_ref[0], ci_ref[0]

    eac = jnp.minimum(arc_ref[0], -1e-4) * dt
    thc = aic_ref[0] * dt
    tau = lax.broadcasted_iota(I32, (1, L), 1).astype(F32)
    m0 = jnp.exp(eac * tau)
    pw0r, pw0i = m0 * jnp.cos(thc * tau), m0 * jnp.sin(thc * tau)
    m1 = jnp.exp(eac * (tau + 1.0))
    pw1r, pw1i = m1 * jnp.cos(thc * (tau + 1.0)), m1 * jnp.sin(thc * (tau + 1.0))
    back = (L - 1.0) - lax.broadcasted_iota(I32, (L, 1), 0).astype(F32)
    mb = jnp.exp(ea * back)
    pbr, pbi = mb * jnp.cos(th * back), mb * jnp.sin(th * back)
    ml = jnp.exp(ea * float(L))
    alr, ali = ml * jnp.cos(th * float(L)), ml * jnp.sin(th * float(L))

    upper = lax.broadcasted_iota(I32, (L, L), 1) >= lax.broadcasted_iota(I32, (L, L), 0)
    for ci in range(C):
        cbr = bbr[ci:ci + 1, :] * crr - bbi[ci:ci + 1, :] * cri
        cbi = bbr[ci:ci + 1, :] * cri + bbi[ci:ci + 1, :] * crr
        kv_ref[ci * C:(ci + 1) * C, :] = (jnp.dot(cbr, pw0r, precision=hi, preferred_element_type=F32)
                                          - jnp.dot(cbi, pw0i, precision=hi, preferred_element_type=F32))
        pr_ref[ci * L:(ci + 1) * L, :] = (pbr * bbr[ci:ci + 1, :] - pbi * bbi[ci:ci + 1, :]).astype(BF16)
        pi_ref[ci * L:(ci + 1) * L, :] = (pbr * bbi[ci:ci + 1, :] + pbi * bbr[ci:ci + 1, :]).astype(BF16)
        u2_ref[:, ci * L:(ci + 1) * L] = u_ref[ci]
    for co in range(C):
        cc_r, cc_i = ctr_ref[0, co], cti_ref[0, co]
        qr_ref[:, co * L:(co + 1) * L] = (cc_r * pw1r - cc_i * pw1i).astype(BF16)
        qi_ref[:, co * L:(co + 1) * L] = (-(cc_r * pw1i + cc_i * pw1r)).astype(BF16)

    def toeplitz(ci, carry):
        for co in range(C):
            taps = kv_ref[pl.ds(ci * C + co, 1), :]
            t = pltpu.roll(jnp.broadcast_to(taps, (L, L)), 0, 1, stride=1, stride_axis=0)
            m_ref[pl.ds(pl.multiple_of(ci * L, L), L), co * L:(co + 1) * L] = jnp.where(upper, t, 0.0).astype(BF16)
        return carry

    lax.fori_loop(0, C, toeplitz, 0)

    u2 = u2_ref[...]
    sr_ref[...] = _dot(u2, pr_ref[...]).reshape(nbatch, nchunk, LANES)
    si_ref[...] = _dot(u2, pi_ref[...]).reshape(nbatch, nchunk, LANES)

    def chunk_scan(c, carry):
        xr, xi = carry
        xr_ref[:, pl.ds(c, 1), :] = xr
        xi_ref[:, pl.ds(c, 1), :] = xi
        nr = alr * xr - ali * xi + sr_ref[:, pl.ds(c, 1), :]
        ni = alr * xi + ali * xr + si_ref[:, pl.ds(c, 1), :]
        return nr, ni

    zero = jnp.zeros((nbatch, 1, LANES), F32)
    lax.fori_loop(0, nchunk, chunk_scan, (zero, zero))
    xpr = xr_ref[...].reshape(N, LANES).astype(BF16)
    xpi = xi_ref[...].reshape(N, LANES).astype(BF16)

    wide = 4 * L
    for n0 in range(C * L // wide):
        y = (_dot(u2, m_ref[:, n0 * wide:(n0 + 1) * wide])
             + _dot(xpr, qr_ref[:, n0 * wide:(n0 + 1) * wide])
             + _dot(xpi, qi_ref[:, n0 * wide:(n0 + 1) * wide]))
        for cc in range(wide // L):
            co = n0 * (wide // L) + cc
            y_ref[co] = (y[:, cc * L:(cc + 1) * L] + d_ref[0, co] * u_ref[co].astype(F32)).astype(BF16)


def _ssm(zt3, row0, params, nbatch):
    (ar, ai, ldt, arc, aic, btr, bti, cr, ci, ctr, cti, d) = params
    G = ar.shape[0]
    C, L = SSM_GROUP, SSM_CHUNK
    N = zt3.shape[1]
    g3 = lambda g: (g, 0, 0)
    g4 = lambda g: (g, 0, 0, 0)
    return pl.pallas_call(
        functools.partial(_ssm_kernel, nbatch=nbatch),
        grid=(G,),
        in_specs=[
            pl.BlockSpec((C, N, L), lambda g: (row0 // C + g, 0, 0)),
            pl.BlockSpec((1, 1, LANES), g3), pl.BlockSpec((1, 1, LANES), g3), pl.BlockSpec((1, 1, 1), g3),
            pl.BlockSpec((1, LANES, 1), g3), pl.BlockSpec((1, LANES, 1), g3),
            pl.BlockSpec((1, C, LANES), g3), pl.BlockSpec((1, C, LANES), g3),
            pl.BlockSpec((1, C, LANES), g3), pl.BlockSpec((1, C, LANES), g3),
            pl.BlockSpec((1, C, LANES, 1), g4), pl.BlockSpec((1, C, LANES, 1), g4),
            pl.BlockSpec((1, C, 1, 1), g4),
        ],
        out_specs=pl.BlockSpec((C, N, L), g3),
        out_shape=jax.ShapeDtypeStruct((G * C, N, L), BF16),
        scratch_shapes=[
            pltpu.VMEM((N, C * L), BF16),
            pltpu.VMEM((C * L, C * L), BF16),
            pltpu.VMEM((C * L, LANES), BF16), pltpu.VMEM((C * L, LANES), BF16),
            pltpu.VMEM((LANES, C * L), BF16), pltpu.VMEM((LANES, C * L), BF16),
            pltpu.VMEM((C * C, L), F32),
            pltpu.VMEM((nbatch, N // nbatch, LANES), F32), pltpu.VMEM((nbatch, N // nbatch, LANES), F32),
            pltpu.VMEM((nbatch, N // nbatch, LANES), F32), pltpu.VMEM((nbatch, N // nbatch, LANES), F32),
        ],
        compiler_params=_cparams(("arbitrary",)),
        name="ssm",
    )(zt3, ar, ai, ldt, arc, aic, btr, bti, cr, ci, ctr, cti, d)


def _mixer_out_kernel(ga_ref, gb_ref, ys_ref, at_ref, x_ref, wglu_ref, wpa_ref, wout_ref, h_ref):
    dm = ga_ref.shape[0]
    nblk = at_ref.shape[1]
    att = jnp.concatenate(
        [jnp.concatenate([at_ref[h, j] for j in range(nblk)], axis=1) for h in range(ATT_HEADS)], axis=0)
    ya = _dot(wpa_ref[...], att)
    gl = jax.nn.gelu(ys_ref[...].astype(F32)).astype(BF16)
    zb = _dot(wglu_ref[...], gl)
    yb = zb[:dm] * jax.nn.sigmoid(zb[dm:])
    merged = jax.nn.sigmoid(ga_ref[...].astype(F32)) * ya + jax.nn.sigmoid(gb_ref[...].astype(F32)) * yb
    h_ref[...] = x_ref[...] + _dot_tn(merged.astype(BF16), wout_ref[...])


def _mixer_out(zt, ys_t, att4, x2, wglu_t, wpa_t, wout, tm=512):
    T, D = x2.shape
    nblk = tm // MOBA_BLOCK
    sw = ys_t.shape[0]
    full = lambda a: pl.BlockSpec(a.shape, lambda i: (0,) * a.ndim)
    return pl.pallas_call(
        _mixer_out_kernel,
        grid=(T // tm,),
        in_specs=[
            pl.BlockSpec((D, tm), lambda i: (0, i)),
            pl.BlockSpec((D, tm), lambda i: (1, i)),
            pl.BlockSpec((sw, tm), lambda i: (0, i)),
            pl.BlockSpec((ATT_HEADS, nblk, ATT_HEAD_DIM, MOBA_BLOCK), lambda i: (0, i, 0, 0)),
            pl.BlockSpec((tm, D), lambda i: (i, 0)),
            full(wglu_t), full(wpa_t), full(wout),
        ],
        out_specs=pl.BlockSpec((tm, D), lambda i: (i, 0)),
        out_shape=jax.ShapeDtypeStruct((T, D), F32),
        compiler_params=_cparams(("arbitrary",)),
        name="mixer_out",
    )(zt, zt, ys_t, att4, x2, wglu_t, wpa_t, wout)


def _memkv_kernel(mem_ref, g_ref, w_ref, kv_ref):
    kv_ref[...] = _dot(_rmsnorm(mem_ref[...], g_ref[...]).astype(BF16), w_ref[...]).astype(BF16)


def _memkv(mem2, g, w, rows):
    M, D = mem2.shape
    return pl.pallas_call(
        _memkv_kernel,
        grid=(M // rows,),
        in_specs=[pl.BlockSpec((rows, D), lambda i: (i, 0)), pl.BlockSpec((1, D), lambda i: (0, 0)),
                  pl.BlockSpec(w.shape, lambda i: (0, 0))],
        out_specs=pl.BlockSpec((rows, w.shape[1]), lambda i: (i, 0)),
        out_shape=jax.ShapeDtypeStruct((M, w.shape[1]), BF16),
        compiler_params=_cparams(("arbitrary",)),
        name="memkv",
    )(mem2, g, w)


def _xattn_kernel(h_ref, g_ref, wq_ref, kv_ref, wo_ref, o_ref):
    h = h_ref[...]
    q = _dot(_rmsnorm(h, g_ref[...]).astype(BF16), wq_ref[...]).astype(BF16)
    kv = kv_ref[...]
    xw = XATTN_HEADS * XATTN_HEAD_DIM
    scale = XATTN_HEAD_DIM ** -0.5
    outs = []
    for hd in range(XATTN_HEADS):
        sl = slice(hd * XATTN_HEAD_DIM, (hd + 1) * XATTN_HEAD_DIM)
        s = _dot_nt(q[:, sl], kv[:, sl]) * scale
        p = jnp.exp(s - jnp.max(s, axis=-1, keepdims=True))
        p = p / jnp.sum(p, axis=-1, keepdims=True)
        outs.append(_dot(p.astype(BF16), kv[:, xw + hd * XATTN_HEAD_DIM:xw + (hd + 1) * XATTN_HEAD_DIM]))
    o = jnp.concatenate(outs, axis=1).astype(BF16)
    o_ref[...] = h + _dot(o, wo_ref[...])


def _xattn(h1, g, wq, kv, wo, seq, mem_len, tm=512):
    T, D = h1.shape
    per_b = seq // tm
    return pl.pallas_call(
        _xattn_kernel,
        grid=(T // tm,),
        in_specs=[
            pl.BlockSpec((tm, D), lambda i: (i, 0)),
            pl.BlockSpec((1, D), lambda i: (0, 0)),
            pl.BlockSpec(wq.shape, lambda i: (0, 0)),
            pl.BlockSpec((mem_len, kv.shape[1]), lambda i: (i // per_b, 0)),
            pl.BlockSpec(wo.shape, lambda i: (0, 0)),
        ],
        out_specs=pl.BlockSpec((tm, D), lambda i: (i, 0)),
        out_shape=jax.ShapeDtypeStruct((T, D), F32),
        compiler_params=_cparams(("arbitrary",)),
        name="xattn",
    )(h1, g, wq, kv, wo)


def _router_kernel(h_ref, g_ref, wr_ref, br_ref, xn_ref, e_ref, r_ref, gate_ref, cnt_ref, carry_ref):
    i = pl.program_id(0)
    E = N_EXPERTS
    tm = h_ref.shape[0]

    @pl.when(i == 0)
    def _():
        carry_ref[...] = jnp.zeros_like(carry_ref)

    n = _rmsnorm(h_ref[...], g_ref[...])
    _to_row_tiles(xn_ref, n)
    na, nb_, nc = _split3(n)
    wa, wb, wc = _split3(wr_ref[...])
    logits = (_dot_nt(wa, na) + (_dot_nt(wa, nb_) + _dot_nt(wb, na))
              + (_dot_nt(wb, nb_) + _dot_nt(wa, nc) + _dot_nt(wc, na))) + br_ref[...]
    eidx = lax.broadcasted_iota(I32, (E, tm), 0)
    g = logits
    picks, vals = [], []
    for _ in range(EXPERT_TOPK):
        mx = jnp.max(g, axis=0, keepdims=True)
        first = jnp.min(jnp.where(g == mx, eidx, E), axis=0, keepdims=True)
        pick = eidx == first
        picks.append(pick)
        vals.append(mx)
        g = jnp.where(pick, -3e38, g)
    ex = [jnp.exp(v - vals[0]) for v in vals]
    tot = ex[0] + ex[1] + ex[2] + ex[3]
    sel = jnp.zeros((E, tm), F32)
    for pk in picks:
        sel = jnp.where(pk, 1.0, sel)
    before = (lax.broadcasted_iota(I32, (tm, tm), 0) < lax.broadcasted_iota(I32, (tm, tm), 1))
    prefix = _dot(sel.astype(BF16), jnp.where(before, 1.0, 0.0).astype(BF16))
    pos = prefix + carry_ref[:, 0:1]
    zi = jnp.zeros((8 - EXPERT_TOPK, tm), I32)
    e_rows = [jnp.sum(jnp.where(pk, eidx, 0), axis=0, keepdims=True) for pk in picks]
    r_rows = [jnp.sum(jnp.where(pk, pos, 0.0), axis=0, keepdims=True).astype(I32) for pk in picks]
    e_ref[...] = jnp.concatenate(e_rows + [zi], axis=0)
    r_ref[...] = jnp.concatenate(r_rows + [zi], axis=0)
    gate_rows = jnp.concatenate([x / tot for x in ex] + [jnp.zeros((LANES - EXPERT_TOPK, tm), F32)], axis=0)
    gate_ref[...] = gate_rows.T
    carry_ref[...] = carry_ref[...] + jnp.sum(sel, axis=1, keepdims=True)
    cnt_ref[...] = carry_ref[...]


def _router(h2, g, wr_t, br, tm=256):
    T, D = h2.shape
    E = N_EXPERTS
    return pl.pallas_call(
        _router_kernel,
        grid=(T // tm,),
        in_specs=[pl.BlockSpec((tm, D), lambda i: (i, 0)), pl.BlockSpec((1, D), lambda i: (0, 0)),
                  pl.BlockSpec((E, D), lambda i: (0, 0)), pl.BlockSpec((E, 1), lambda i: (0, 0))],
        out_specs=[pl.BlockSpec((tm * ROW_TILE, LANES), lambda i: (i, 0)),
                   pl.BlockSpec((8, tm), lambda i: (0, i)), pl.BlockSpec((8, tm), lambda i: (0, i)),
                   pl.BlockSpec((tm, LANES), lambda i: (i, 0)),
                   pl.BlockSpec((E, LANES), lambda i: (0, 0))],
        out_shape=[jax.ShapeDtypeStruct((T * ROW_TILE, LANES), F32),
                   jax.ShapeDtypeStruct((8, T), I32), jax.ShapeDtypeStruct((8, T), I32),
                   jax.ShapeDtypeStruct((T, LANES), F32),
                   jax.ShapeDtypeStruct((E, LANES), F32)],
        scratch_shapes=[pltpu.VMEM((E, LANES), F32)],
        compiler_params=_cparams(("arbitrary",)),
        name="router",
    )(h2, g, wr_t, br)


def _positions_kernel(ps_ref, e_ref, r_ref, pos_ref):
    e = e_ref[...]
    pos = r_ref[...]
    for x in range(N_EXPERTS):
        pos = pos + jnp.where(e == x, ps_ref[x], 0)
    for c in range(pos_ref.shape[0]):
        pos_ref[c] = pos[:, c * LANES:(c + 1) * LANES]


def _positions(pstart, e8, r8, tm=2048):
    T = e8.shape[1]
    grid_spec = pltpu.PrefetchScalarGridSpec(
        num_scalar_prefetch=1,
        grid=(T // tm,),
        in_specs=[pl.BlockSpec((8, tm), lambda i, ps: (0, i)), pl.BlockSpec((8, tm), lambda i, ps: (0, i))],
        out_specs=pl.BlockSpec((tm // LANES, 8, LANES), lambda i, ps: (i, 0, 0)),
    )
    return pl.pallas_call(
        _positions_kernel, grid_spec=grid_spec, out_shape=jax.ShapeDtypeStruct((T // LANES, 8, LANES), I32),
        compiler_params=_cparams(("arbitrary",)), name="positions",
    )(pstart, e8, r8)


def _row_dmas(tm, make_copy, wait):
    for c in range(tm // LANES):
        for k0 in range(0, EXPERT_TOPK, 2):
            def body(lane, carry, c=c, k0=k0):
                for k in (k0, k0 + 1):
                    cp = make_copy(c, k, lane)
                    cp.wait() if wait else cp.start(priority=k % 2)
                return carry
            lax.fori_loop(0, LANES, body, 0, unroll=4)


def _dispatch_kernel(zr_ref, pos_ref, xn_ref, xs_ref, zbuf, sem, zsem):
    tm = xn_ref.shape[0] // ROW_TILE

    @pl.when(pl.program_id(0) == 0)
    def _():
        zbuf[...] = jnp.zeros_like(zbuf)

        def zero_copy(e):
            first = pl.multiple_of(zr_ref[e] * ROW_TILE, MOE_BLOCK * ROW_TILE)
            return pltpu.make_async_copy(zbuf, xs_ref.at[pl.ds(first, MOE_BLOCK * ROW_TILE)], zsem)

        for e in range(2 * N_EXPERTS):
            @pl.when(zr_ref[e] >= 0)
            def _():
                zero_copy(e).start()
        for e in range(2 * N_EXPERTS):
            @pl.when(zr_ref[e] >= 0)
            def _():
                zero_copy(e).wait()

    row_copy = lambda c, k, lane: pltpu.make_async_copy(
        _tile(xn_ref, c * LANES + lane), _tile(xs_ref, _pos(pos_ref, c, k, lane)), sem)
    _row_dmas(tm, row_copy, wait=False)
    _row_dmas(tm, row_copy, wait=True)


def _dispatch(zrow, pos3, xn, rows_total, tm=1024):
    T = xn.shape[0] // ROW_TILE
    grid_spec = pltpu.PrefetchScalarGridSpec(
        num_scalar_prefetch=1,
        grid=(T // tm,),
        in_specs=[pl.BlockSpec((tm * 8,), lambda i, zr: (i,), memory_space=pltpu.SMEM),
                  pl.BlockSpec((tm * ROW_TILE, LANES), lambda i, zr: (i, 0))],
        out_specs=pl.BlockSpec(memory_space=pl.ANY),
        scratch_shapes=[pltpu.VMEM((MOE_BLOCK * ROW_TILE, LANES), F32), pltpu.SemaphoreType.DMA(()),
                        pltpu.SemaphoreType.DMA(())],
    )
    return pl.pallas_call(
        _dispatch_kernel,
        grid_spec=grid_spec,
        out_shape=jax.ShapeDtypeStruct((rows_total * ROW_TILE, LANES), F32),
        compiler_params=_cparams(("arbitrary",)),
        name="dispatch",
    )(zrow, pos3, xn)


def _experts_kernel(be_ref, nu_ref, xs_ref, wgu_ref, bgu_ref, wdn_ref, bdn_ref, ys_ref, wgu_bf, wdn_bf):
    i = pl.program_id(0)
    dff = wdn_ref.shape[1]

    @pl.when(i < nu_ref[0])
    def _():
        prev = be_ref[jnp.maximum(i - 1, 0)]

        @pl.when((i == 0) | (be_ref[i] != prev))
        def _():
            wgu_bf[...] = wgu_ref[0].astype(BF16)
            wdn_bf[...] = wdn_ref[0].astype(BF16)

        gu = _dot(_from_row_tiles(xs_ref, MOE_BLOCK).astype(BF16), wgu_bf[...]) + bgu_ref[0]
        gate = jnp.minimum(gu[:, :dff], SWIGLU_LIMIT)
        up = jnp.clip(gu[:, dff:], -SWIGLU_LIMIT, SWIGLU_LIMIT)
        hid = (up + 1.0) * gate * jax.nn.sigmoid(SWIGLU_ALPHA * gate)
        _to_row_tiles(ys_ref, _dot(hid.astype(BF16), wdn_bf[...]) + bdn_ref[0])

    @pl.when(i >= nu_ref[0])
    def _():
        ys_ref[...] = jnp.zeros_like(ys_ref)


def _experts(blk_expert, nused, xs, wgu, bgu, wdn, bdn):
    rows = xs.shape[0] // ROW_TILE
    E, D, F2 = wgu.shape
    dff = wdn.shape[1]
    nblk = rows // MOE_BLOCK
    grid_spec = pltpu.PrefetchScalarGridSpec(
        num_scalar_prefetch=2,
        grid=(nblk,),
        in_specs=[pl.BlockSpec((MOE_BLOCK * ROW_TILE, LANES), lambda i, be, nu: (jnp.minimum(i, nu[0] - 1), 0)),
                  pl.BlockSpec((1, D, F2), lambda i, be, nu: (be[i], 0, 0)),
                  pl.BlockSpec((1, 1, F2), lambda i, be, nu: (be[i], 0, 0)),
                  pl.BlockSpec((1, dff, D), lambda i, be, nu: (be[i], 0, 0)),
                  pl.BlockSpec((1, 1, D), lambda i, be, nu: (be[i], 0, 0))],
        out_specs=pl.BlockSpec((MOE_BLOCK * ROW_TILE, LANES), lambda i, be, nu: (i, 0)),
        scratch_shapes=[pltpu.VMEM((D, F2), BF16), pltpu.VMEM((dff, D), BF16)],
    )
    return pl.pallas_call(
        _experts_kernel,
        grid_spec=grid_spec,
        out_shape=jax.ShapeDtypeStruct((rows * ROW_TILE, LANES), F32),
        compiler_params=_cparams(("arbitrary",)),
        name="experts",
    )(blk_expert, nused, xs, wgu, bgu, wdn, bdn)


def _combine_kernel(pos_ref, nxt_ref, h_ref, gate_ref, gf_ref, ys_ref, o_ref, buf, sem):
    i = pl.program_id(0)
    tm = h_ref.shape[0]
    slot = i % 2

    def gather(p_ref, slot):
        first = slot * (EXPERT_TOPK * tm)
        return lambda c, k, lane: pltpu.make_async_copy(
            _tile(ys_ref, _pos(p_ref, c, k, lane)), _tile(buf, first + k * tm + c * LANES + lane), sem.at[slot])

    @pl.when(i == 0)
    def _():
        _row_dmas(tm, gather(pos_ref, slot), wait=False)

    @pl.when(i + 1 < pl.num_programs(0))
    def _():
        _row_dmas(tm, gather(nxt_ref, 1 - slot), wait=False)

    _row_dmas(tm, gather(pos_ref, slot), wait=True)
    gates = gate_ref[...]
    acc = h_ref[...]
    for k in range(EXPERT_TOPK):
        acc = acc + gates[:, k:k + 1] * _from_row_tiles(buf, tm, first=slot * (EXPERT_TOPK * tm) + k * tm)
    o_ref[...] = _rmsnorm(acc, gf_ref[...])


def _combine(pos3, h2, gates, gf, ys, tm=256):
    T, D = h2.shape
    return pl.pallas_call(
        _combine_kernel,
        grid=(T // tm,),
        in_specs=[pl.BlockSpec((tm * 8,), lambda i: (i,), memory_space=pltpu.SMEM),
                  pl.BlockSpec((tm * 8,), lambda i: (jnp.minimum(i + 1, T // tm - 1),), memory_space=pltpu.SMEM),
                  pl.BlockSpec((tm, D), lambda i: (i, 0)),
                  pl.BlockSpec((tm, LANES), lambda i: (i, 0)),
                  pl.BlockSpec((1, D), lambda i: (0, 0)),
                  pl.BlockSpec(memory_space=pl.ANY)],
        out_specs=pl.BlockSpec((tm, D), lambda i: (i, 0)),
        out_shape=jax.ShapeDtypeStruct((T, D), F32),
        scratch_shapes=[pltpu.VMEM((2 * EXPERT_TOPK * tm * ROW_TILE, LANES), F32), pltpu.SemaphoreType.DMA((2,))],
        compiler_params=_cparams(("arbitrary",)),
        name="combine",
    )(pos3, pos3, h2, gates, gf, ys)


def _pad_lanes(a, width=LANES, value=0.0):
    pad = [(0, 0)] * (a.ndim - 1) + [(0, width - a.shape[-1])]
    return jnp.pad(a, pad, constant_values=value)


def _layer(h2d, mem2d, batch, seq, mem_len, g_mix, w_in, a_re, a_im, log_dt, b_re, b_im, c_re, c_im, d_skip,
           w_glu, w_proj_a, w_out, g_xattn, g_mem, w_xq, w_xkv, w_xo, g_moe, w_router, b_router,
           w_gu, b_gu, w_dn, b_dn, g_final):
    T, D = h2d.shape
    aw = ATT_HEADS * ATT_HEAD_DIM
    G = a_re.shape[0]
    sw = G * SSM_GROUP

    wq, wk, wv, wu, wga, wgb = jnp.split(w_in, [aw, 2 * aw, 3 * aw, 3 * aw + sw, 3 * aw + sw + D], axis=1)
    wg_t = jnp.concatenate([wga, wgb, wu], axis=1).T.astype(BF16)
    wqv_t = jnp.concatenate([wq * (ATT_HEAD_DIM ** -0.5), wv], axis=1).T.astype(BF16)
    wk_pad = _pad_lanes(wk.reshape(D, ATT_HEADS, ATT_HEAD_DIM)).reshape(D, ATT_HEADS * LANES).astype(BF16)

    zt, q4, v4, kpad, km = _in_proj(h2d, g_mix.reshape(1, D), wg_t, wqv_t, wk_pad)
    nb = seq // MOBA_BLOCK
    att4 = _moba(q4, v4, kpad.reshape(T // MOBA_BLOCK, MOBA_BLOCK, ATT_HEADS * LANES),
                 km.reshape(batch, nb, ATT_HEADS * LANES), batch)

    row = lambda a, v=0.0: _pad_lanes(a, value=v).reshape(G, 1, LANES)
    col = lambda a, v=0.0: _pad_lanes(a, value=v).reshape(G, LANES, 1)
    ssm_params = (
        row(a_re, -1.0), row(a_im), log_dt.reshape(G, 1, 1), col(a_re, -1.0), col(a_im),
        _pad_lanes(jnp.swapaxes(b_re, 1, 2)), _pad_lanes(jnp.swapaxes(b_im, 1, 2)),
        _pad_lanes(c_re), _pad_lanes(c_im),
        _pad_lanes(c_re)[..., None], _pad_lanes(c_im)[..., None],
        d_skip.reshape(G, SSM_GROUP, 1, 1),
    )
    ys3 = _ssm(zt.reshape(zt.shape[0], T // SSM_CHUNK, SSM_CHUNK), 2 * D, ssm_params, batch)
    h1 = _mixer_out(zt, ys3.reshape(sw, T), att4, h2d, w_glu.T.astype(BF16), w_proj_a.T.astype(BF16),
                    w_out.astype(BF16))

    kv = _memkv(mem2d, g_mem.reshape(1, D), w_xkv.astype(BF16), mem_len)
    h2 = _xattn(h1, g_xattn.reshape(1, D), w_xq.astype(BF16), kv, w_xo.astype(BF16), seq, mem_len)

    xn, e8, r8, gates, cnt = _router(h2, g_moe.reshape(1, D), w_router.T, b_router.reshape(N_EXPERTS, 1))
    counts = cnt[:, 0].astype(I32)
    nblk_e = (counts + MOE_BLOCK - 1) // MOE_BLOCK
    bends = jnp.cumsum(nblk_e)
    pstart = ((bends - nblk_e) * MOE_BLOCK).astype(I32)
    nblk = (T * EXPERT_TOPK) // MOE_BLOCK + N_EXPERTS
    blk_expert = jnp.minimum(jnp.sum(bends[None, :] <= jnp.arange(nblk, dtype=I32)[:, None], axis=1),
                             N_EXPERTS - 1).astype(I32)
    nused = bends[-1:].astype(I32)
    blk_expert = jnp.where(jnp.arange(nblk) < nused[0], blk_expert, blk_expert[jnp.maximum(nused[0] - 1, 0)])

    tail = nblk - 1 - jnp.arange(N_EXPERTS, dtype=I32)
    zrow = jnp.concatenate([jnp.where(nblk_e > 0, (bends - 1) * MOE_BLOCK, -1),
                            jnp.where(tail >= nused[0], tail * MOE_BLOCK, -1)]).astype(I32)
    pos3 = _positions(pstart, e8, r8).reshape(-1)
    xs = _dispatch(zrow, pos3, xn, nblk * MOE_BLOCK)
    ys = _experts(blk_expert, nused, xs, w_gu, b_gu.reshape(N_EXPERTS, 1, -1), w_dn, b_dn.reshape(N_EXPERTS, 1, -1))
    return _combine(pos3, h2, gates, g_final.reshape(1, D), ys)


def kernel(x, mem, g_mix, w_in, a_re, a_im, log_dt, b_re, b_im, c_re, c_im, d_skip, w_glu, w_proj_a, w_out,
           g_xattn, g_mem, w_xq, w_xkv, w_xo, g_moe, w_router, b_router, w_gu, b_gu, w_dn, b_dn, g_final):
    B, S, D = x.shape
    M = mem.shape[1]
    assert g_mix.shape[0] == 1, "single layer"
    out = _layer(x.reshape(B * S, D), mem.reshape(B * M, D), B, S, M,
                 g_mix[0], w_in[0], a_re[0], a_im[0], log_dt[0], b_re[0], b_im[0], c_re[0], c_im[0], d_skip[0],
                 w_glu[0], w_proj_a[0], w_out[0], g_xattn[0], g_mem[0], w_xq[0], w_xkv[0], w_xo[0],
                 g_moe[0], w_router[0], b_router[0], w_gu[0], b_gu[0], w_dn[0], b_dn[0], g_final)
    return out.reshape(B, S, D)
```

```python
import functools
import math

import jax
import jax.numpy as jnp
from jax import lax
from jax.experimental import pallas as pl
from jax.experimental.pallas import tpu as pltpu

F32 = jnp.float32
BF16 = jnp.bfloat16
I32 = jnp.int32

RMS_EPS = 1e-6
NEG_INF = -1e30
ATT_HEADS = 8
ATT_HEAD_DIM = 64
MOBA_BLOCK = 256
MOBA_TOPK = 3
MOBA_HEADS_PER_STEP = 4
SSM_GROUP = 16
SSM_STATE = 64
SSM_CHUNK = 128
XATTN_HEADS = 4
XATTN_HEAD_DIM = 128
N_EXPERTS = 32
EXPERT_TOPK = 4
SWIGLU_LIMIT = 7.0
SWIGLU_ALPHA = 1.702
MOE_BLOCK = 512
LANES = 128
VMEM_LIMIT = 56 * 1024 * 1024


def _cparams(sem):
    return pltpu.CompilerParams(dimension_semantics=sem, vmem_limit_bytes=VMEM_LIMIT)


def _rmsnorm(x, g):
    return x * lax.rsqrt(jnp.mean(x * x, axis=-1, keepdims=True) + RMS_EPS) * g


def _dot(a, b):
    return jnp.dot(a, b, preferred_element_type=F32)


def _dot_nt(a, b):
    return lax.dot_general(a, b, (((1,), (1,)), ((), ())), preferred_element_type=F32)


def _dot_tn(a, b):
    return lax.dot_general(a, b, (((0,), (0,)), ((), ())), preferred_element_type=F32)


def _split3(x):
    a = x.astype(BF16)
    r = x - a.astype(F32)
    b = r.astype(BF16)
    c = (r - b.astype(F32)).astype(BF16)
    return a, b, c


ROW_TILE = 8


def _to_row_tiles(ref, x, first=0):
    for s in range(ROW_TILE):
        ref[pl.ds(first * ROW_TILE + s, x.shape[0], stride=ROW_TILE), :] = x[:, s * LANES:(s + 1) * LANES]


def _from_row_tiles(ref, rows, first=0):
    return jnp.concatenate(
        [ref[pl.ds(first * ROW_TILE + s, rows, stride=ROW_TILE), :] for s in range(ROW_TILE)], axis=1)


def _pos(pos_ref, c, k, lane):
    return pos_ref[(c * 8 + k) * LANES + lane]


def _tile(ref, r):
    return ref.at[pl.ds(pl.multiple_of(r * ROW_TILE, ROW_TILE), ROW_TILE)]


def _inproj_kernel(x_ref, g_ref, wg_ref, wqv_ref, wk_ref, zt_ref, q_ref, v_ref, k_ref, km_ref):
    n = _rmsnorm(x_ref[...], g_ref[...]).astype(BF16)
    tm = n.shape[0]
    nblk = tm // MOBA_BLOCK
    rows = 512
    for c in range(wg_ref.shape[0] // rows):
        zt_ref[c * rows:(c + 1) * rows, :] = _dot_nt(wg_ref[c * rows:(c + 1) * rows, :], n).astype(BF16)
    for c, dst in ((0, q_ref), (1, v_ref)):
        r = _dot_nt(wqv_ref[c * rows:(c + 1) * rows, :], n).astype(BF16)
        for h in range(ATT_HEADS):
            for j in range(nblk):
                dst[h, j] = r[h * ATT_HEAD_DIM:(h + 1) * ATT_HEAD_DIM, j * MOBA_BLOCK:(j + 1) * MOBA_BLOCK]
    k = _dot(n, wk_ref[...])
    lane = lax.broadcasted_iota(I32, (1, k.shape[1]), 1) & (LANES - 1)
    k = k + jnp.where(lane == ATT_HEAD_DIM, 1.0, 0.0)
    k_ref[...] = k.astype(BF16)
    km_ref[0] = jnp.mean(k.reshape(nblk, MOBA_BLOCK, k.shape[1]), axis=1)


def _in_proj(x2, g, wg_t, wqv_t, wk_pad, tm=512):
    T, D = x2.shape
    nblk = tm // MOBA_BLOCK
    ng = wg_t.shape[0]
    kw = wk_pad.shape[1]
    hd = (ATT_HEADS, T // MOBA_BLOCK, ATT_HEAD_DIM, MOBA_BLOCK)
    return pl.pallas_call(
        _inproj_kernel,
        grid=(T // tm,),
        in_specs=[
            pl.BlockSpec((tm, D), lambda i: (i, 0)),
            pl.BlockSpec((1, D), lambda i: (0, 0)),
            pl.BlockSpec((ng, D), lambda i: (0, 0)),
            pl.BlockSpec(wqv_t.shape, lambda i: (0, 0)),
            pl.BlockSpec((D, kw), lambda i: (0, 0)),
        ],
        out_specs=[
            pl.BlockSpec((ng, tm), lambda i: (0, i)),
            pl.BlockSpec((ATT_HEADS, nblk, ATT_HEAD_DIM, MOBA_BLOCK), lambda i: (0, i, 0, 0)),
            pl.BlockSpec((ATT_HEADS, nblk, ATT_HEAD_DIM, MOBA_BLOCK), lambda i: (0, i, 0, 0)),
            pl.BlockSpec((tm, kw), lambda i: (i, 0)),
            pl.BlockSpec((1, nblk, kw), lambda i: (i, 0, 0)),
        ],
        out_shape=[
            jax.ShapeDtypeStruct((ng, T), BF16),
            jax.ShapeDtypeStruct(hd, BF16),
            jax.ShapeDtypeStruct(hd, BF16),
            jax.ShapeDtypeStruct((T, kw), BF16),
            jax.ShapeDtypeStruct((T // tm, nblk, kw), F32),
        ],
        compiler_params=_cparams(("arbitrary",)),
        name="in_proj",
    )(x2, g, wg_t, wqv_t, wk_pad)


def _moba_kernel(q_ref, v_ref, k_ref, km_ref, o_ref, sel_ref, acc_ref, sa_ref, sb_ref, pb_ref, *, nb, hb):
    blk = MOBA_BLOCK
    heads = range(hb)
    lanes = lambda h: slice(h * LANES, (h + 1) * LANES)
    kms = [_split3(km_ref[0][:, lanes(h)]) for h in heads]
    zpad = jnp.zeros((LANES - ATT_HEAD_DIM, blk), BF16)
    bidx = lax.broadcasted_iota(I32, (nb, blk), 0)
    causal = lax.broadcasted_iota(I32, (blk, blk), 0) <= lax.broadcasted_iota(I32, (blk, blk), 1)
    topk = min(MOBA_TOPK, nb)

    row16 = lax.broadcasted_iota(I32, (16, blk), 0) == 0
    ones16 = jnp.where(row16, 1.0, 0.0).astype(BF16)
    zpad48 = jnp.zeros((LANES - ATT_HEAD_DIM - 16, blk), BF16)

    def qblock(i, carry):
        elig = bidx < i
        q64 = [q_ref[h, i] for h in heads]
        for h in heads:
            q = jnp.concatenate([q64[h], zpad], axis=0)
            gate = (_dot(kms[h][0], q) + _dot(kms[h][1], q)) + _dot(kms[h][2], q)
            g = jnp.where(elig, gate, NEG_INF)
            sel = jnp.zeros((nb, blk), F32)
            for _ in range(topk):
                mx = jnp.max(g, axis=0, keepdims=True)
                first = jnp.min(jnp.where(g == mx, bidx, nb), axis=0, keepdims=True)
                pick = bidx == first
                sel = jnp.where(pick, 1.0, sel)
                g = jnp.where(pick, -3e38, g)
            sel_ref[h] = jnp.where(elig, sel, 0.0)
            acc_ref[h] = jnp.zeros((ATT_HEAD_DIM + 16, blk), F32)
            pb_ref[h] = jnp.zeros((blk, blk), BF16)

        def scores(h, t):
            bias = (sel_ref[h, pl.ds(t, 1), :] - 1.0) * 1e30
            q = jnp.concatenate([q64[h], jnp.where(row16, bias, 0.0).astype(BF16), zpad48], axis=0)
            return _dot(k_ref[t][:, lanes(h)], q)

        def values(h, t, p):
            return _dot(jnp.concatenate([v_ref[h, t], ones16], axis=0), p)

        def softmax(s, m):
            m_new = jnp.maximum(m, jnp.max(s, axis=0, keepdims=True))
            return m_new, jnp.exp(m - m_new), jnp.exp(s - m_new).astype(BF16)

        for h in heads:
            sa_ref[h] = scores(h, 0)

        def kvpair(u, c):
            ta, tb, tn = 2 * u, 2 * u + 1, jnp.minimum(2 * u + 2, nb - 1)
            pv = [values(h, jnp.maximum(ta - 1, 0), pb_ref[h]) for h in heads]
            sb = [scores(h, tb) for h in heads]
            for h in heads:
                sb_ref[h] = sb[h]
                acc_ref[h] = acc_ref[h] * c[2 * h + 1] + pv[h]
            sm = [softmax(sa_ref[h], c[2 * h]) for h in heads]
            pv = [values(h, ta, sm[h][2]) for h in heads]
            sa = [scores(h, tn) for h in heads]
            out = []
            for h in heads:
                sa_ref[h] = sa[h]
                acc_ref[h] = acc_ref[h] * sm[h][1] + pv[h]
                m_new, alpha, p = softmax(sb_ref[h], sm[h][0])
                pb_ref[h] = p
                out += [m_new, alpha]
            return tuple(out)

        m0 = jnp.full((1, blk), -1e29, F32)
        npair = (i + 1) // 2
        c = lax.fori_loop(0, npair, kvpair, (m0, jnp.ones((1, blk), F32)) * hb)
        pv = [values(h, jnp.maximum(2 * npair - 1, 0), pb_ref[h]) for h in heads]
        kd = k_ref[i]
        own = [_dot(kd[:, lanes(h)], jnp.concatenate([q64[h], zpad], axis=0)) for h in heads]
        sm = [softmax(jnp.where(causal, own[h], NEG_INF), c[2 * h]) for h in heads]
        for h in heads:
            res = (acc_ref[h] * c[2 * h + 1] + pv[h]) * sm[h][1] + values(h, i, sm[h][2])
            o_ref[h, i] = (res[:ATT_HEAD_DIM] / res[ATT_HEAD_DIM:ATT_HEAD_DIM + 1]).astype(BF16)
        return carry

    lax.fori_loop(0, nb, qblock, 0)


def _moba(q4, v4, k3, km3, batch, hb=MOBA_HEADS_PER_STEP):
    H, nblk_total, dh, blk = q4.shape
    nb = nblk_total // batch
    return pl.pallas_call(
        functools.partial(_moba_kernel, nb=nb, hb=hb),
        grid=(batch, H // hb),
        in_specs=[
            pl.BlockSpec((hb, nb, dh, blk), lambda b, h: (h, b, 0, 0)),
            pl.BlockSpec((hb, nb, dh, blk), lambda b, h: (h, b, 0, 0)),
            pl.BlockSpec((nb, blk, hb * LANES), lambda b, h: (b, 0, h)),
            pl.BlockSpec((1, nb, hb * LANES), lambda b, h: (b, 0, h)),
        ],
        out_specs=pl.BlockSpec((hb, nb, dh, blk), lambda b, h: (h, b, 0, 0)),
        out_shape=jax.ShapeDtypeStruct(q4.shape, BF16),
        scratch_shapes=[pltpu.VMEM((hb, nb, blk), F32),
                        pltpu.VMEM((hb, dh + 16, blk), F32),
                        pltpu.VMEM((hb, blk, blk), F32),
                        pltpu.VMEM((hb, blk, blk), F32),
                        pltpu.VMEM((hb, blk, blk), BF16)],
        compiler_params=_cparams(("arbitrary", "arbitrary")),
        name="moba",
    )(q4, v4, k3, km3)


def _ssm_kernel(u_ref, ar_ref, ai_ref, ldt_ref, arc_ref, aic_ref, btr_ref, bti_ref, cr_ref, ci_ref,
                ctr_ref, cti_ref, d_ref, y_ref,
                u2_ref, m_ref, pr_ref, pi_ref, qr_ref, qi_ref, kv_ref, sr_ref, si_ref, xr_ref, xi_ref,
                *, nbatch):
    L = SSM_CHUNK
    C = SSM_GROUP
    N = u_ref.shape[1]
    nchunk = N // nbatch
    hi = lax.Precision.HIGHEST

    dt = jnp.exp(ldt_ref[0])
    are = jnp.minimum(ar_ref[0], -1e-4)
    aim = ai_ref[0]
    ea, th = are * dt, aim * dt
    mag = jnp.exp(ea)
    lbr, lbi = mag * jnp.cos(th), mag * jnp.sin(th)
    den = are * are + aim * aim
    cfr = ((lbr - 1.0) * are + lbi * aim) / den
    cfi = (lbi * are - (lbr - 1.0) * aim) / den
    btr, bti = btr_ref[0], bti_ref[0]
    bbr = cfr * btr - cfi * bti
    bbi = cfr * bti + cfi * btr
    crr, cri = cr_ref[0], ci_ref[0]

    eac = jnp.minimum(arc_ref[0], -1e-4) * dt
    thc = aic_ref[0] * dt
    tau = lax.broadcasted_iota(I32, (1, L), 1).astype(F32)
    m0 = jnp.exp(eac * tau)
    pw0r, pw0i = m0 * jnp.cos(thc * tau), m0 * jnp.sin(thc * tau)
    m1 = jnp.exp(eac * (tau + 1.0))
    pw1r, pw1i = m1 * jnp.cos(thc * (tau + 1.0)), m1 * jnp.sin(thc * (tau + 1.0))
    back = (L - 1.0) - lax.broadcasted_iota(I32, (L, 1), 0).astype(F32)
    mb = jnp.exp(ea * back)
    pbr, pbi = mb * jnp.cos(th * back), mb * jnp.sin(th * back)
    ml = jnp.exp(ea * float(L))
    alr, ali = ml * jnp.cos(th * float(L)), ml * jnp.sin(th * float(L))

    upper = lax.broadcasted_iota(I32, (L, L), 1) >= lax.broadcasted_iota(I32, (L, L), 0)
    for ci in range(C):
        cbr = bbr[ci:ci + 1, :] * crr - bbi[ci:ci + 1, :] * cri
        cbi = bbr[ci:ci + 1, :] * cri + bbi[ci:ci + 1, :] * crr
        kv_ref[ci * C:(ci + 1) * C, :] = (jnp.dot(cbr, pw0r, precision=hi, preferred_element_type=F32)
                                          - jnp.dot(cbi, pw0i, precision=hi, preferred_element_type=F32))
        pr_ref[ci * L:(ci + 1) * L, :] = (pbr * bbr[ci:ci + 1, :] - pbi * bbi[ci:ci + 1, :]).astype(BF16)
        pi_ref[ci * L:(ci + 1) * L, :] = (pbr * bbi[ci:ci + 1, :] + pbi * bbr[ci:ci + 1, :]).astype(BF16)
        u2_ref[:, ci * L:(ci + 1) * L] = u_ref[ci]
    for co in range(C):
        cc_r, cc_i = ctr_ref[0, co], cti_ref[0, co]
        qr_ref[:, co * L:(co + 1) * L] = (cc_r * pw1r - cc_i * pw1i).astype(BF16)
        qi_ref[:, co * L:(co + 1) * L] = (-(cc_r * pw1i + cc_i * pw1r)).astype(BF16)

    wide = 4 * L
    nwide = C * L // wide

    def toeplitz(n0):
        for ci in range(C):
            for co in range(n0 * (wide // L), (n0 + 1) * (wide // L)):
                taps = kv_ref[ci * C + co:ci * C + co + 1, :]
                t = pltpu.roll(jnp.broadcast_to(taps, (L, L)), 0, 1, stride=1, stride_axis=0)
                m_ref[ci * L:(ci + 1) * L, co * L:(co + 1) * L] = jnp.where(upper, t, 0.0).astype(BF16)

    u2 = u2_ref[...]
    sr_ref[...] = _dot(u2, pr_ref[...]).reshape(nbatch, nchunk, LANES)
    si_ref[...] = _dot(u2, pi_ref[...]).reshape(nbatch, nchunk, LANES)
    toeplitz(0)

    def chunk_scan(c, carry):
        xr, xi = carry
        xr_ref[:, pl.ds(c, 1), :] = xr
        xi_ref[:, pl.ds(c, 1), :] = xi
        nr = alr * xr - ali * xi + sr_ref[:, pl.ds(c, 1), :]
        ni = alr * xi + ali * xr + si_ref[:, pl.ds(c, 1), :]
        return nr, ni

    zero = jnp.zeros((nbatch, 1, LANES), F32)
    lax.fori_loop(0, nchunk, chunk_scan, (zero, zero))
    xpr = xr_ref[...].reshape(N, LANES).astype(BF16)
    xpi = xi_ref[...].reshape(N, LANES).astype(BF16)

    for n0 in range(nwide):
        y = (_dot(u2, m_ref[:, n0 * wide:(n0 + 1) * wide])
             + _dot(xpr, qr_ref[:, n0 * wide:(n0 + 1) * wide])
             + _dot(xpi, qi_ref[:, n0 * wide:(n0 + 1) * wide]))
        if n0 + 1 < nwide:
            toeplitz(n0 + 1)
        for cc in range(wide // L):
            co = n0 * (wide // L) + cc
            y_ref[co] = (y[:, cc * L:(cc + 1) * L] + d_ref[0, co] * u_ref[co].astype(F32)).astype(BF16)


def _ssm(zt3, row0, params, nbatch):
    (ar, ai, ldt, arc, aic, btr, bti, cr, ci, ctr, cti, d) = params
    G = ar.shape[0]
    C, L = SSM_GROUP, SSM_CHUNK
    N = zt3.shape[1]
    g3 = lambda g: (g, 0, 0)
    g4 = lambda g: (g, 0, 0, 0)
    return pl.pallas_call(
        functools.partial(_ssm_kernel, nbatch=nbatch),
        grid=(G,),
        in_specs=[
            pl.BlockSpec((C, N, L), lambda g: (row0 // C + g, 0, 0)),
            pl.BlockSpec((1, 1, LANES), g3), pl.BlockSpec((1, 1, LANES), g3), pl.BlockSpec((1, 1, 1), g3),
            pl.BlockSpec((1, LANES, 1), g3), pl.BlockSpec((1, LANES, 1), g3),
            pl.BlockSpec((1, C, LANES), g3), pl.BlockSpec((1, C, LANES), g3),
            pl.BlockSpec((1, C, LANES), g3), pl.BlockSpec((1, C, LANES), g3),
            pl.BlockSpec((1, C, LANES, 1), g4), pl.BlockSpec((1, C, LANES, 1), g4),
            pl.BlockSpec((1, C, 1, 1), g4),
        ],
        out_specs=pl.BlockSpec((C, N, L), g3),
        out_shape=jax.ShapeDtypeStruct((G * C, N, L), BF16),
        scratch_shapes=[
            pltpu.VMEM((N, C * L), BF16),
            pltpu.VMEM((C * L, C * L), BF16),
            pltpu.VMEM((C * L, LANES), BF16), pltpu.VMEM((C * L, LANES), BF16),
            pltpu.VMEM((LANES, C * L), BF16), pltpu.VMEM((LANES, C * L), BF16),
            pltpu.VMEM((C * C, L), F32),
            pltpu.VMEM((nbatch, N // nbatch, LANES), F32), pltpu.VMEM((nbatch, N // nbatch, LANES), F32),
            pltpu.VMEM((nbatch, N // nbatch, LANES), F32), pltpu.VMEM((nbatch, N // nbatch, LANES), F32),
        ],
        compiler_params=_cparams(("arbitrary",)),
        name="ssm",
    )(zt3, ar, ai, ldt, arc, aic, btr, bti, cr, ci, ctr, cti, d)


def _mixer_out_kernel(ga_ref, gb_ref, ys_ref, at_ref, x_ref, wglu_ref, wpa_ref, wout_ref, h_ref):
    dm = ga_ref.shape[0]
    nblk = at_ref.shape[1]
    att = jnp.concatenate(
        [jnp.concatenate([at_ref[h, j] for j in range(nblk)], axis=1) for h in range(ATT_HEADS)], axis=0)
    ya = _dot(wpa_ref[...], att)
    gl = jax.nn.gelu(ys_ref[...].astype(F32)).astype(BF16)
    zb = _dot(wglu_ref[...], gl)
    yb = zb[:dm] * jax.nn.sigmoid(zb[dm:])
    merged = jax.nn.sigmoid(ga_ref[...].astype(F32)) * ya + jax.nn.sigmoid(gb_ref[...].astype(F32)) * yb
    h_ref[...] = x_ref[...] + _dot_tn(merged.astype(BF16), wout_ref[...])


def _mixer_out(zt, ys_t, att4, x2, wglu_t, wpa_t, wout, tm=512):
    T, D = x2.shape
    nblk = tm // MOBA_BLOCK
    sw = ys_t.shape[0]
    full = lambda a: pl.BlockSpec(a.shape, lambda i: (0,) * a.ndim)
    return pl.pallas_call(
        _mixer_out_kernel,
        grid=(T // tm,),
        in_specs=[
            pl.BlockSpec((D, tm), lambda i: (0, i)),
            pl.BlockSpec((D, tm), lambda i: (1, i)),
            pl.BlockSpec((sw, tm), lambda i: (0, i)),
            pl.BlockSpec((ATT_HEADS, nblk, ATT_HEAD_DIM, MOBA_BLOCK), lambda i: (0, i, 0, 0)),
            pl.BlockSpec((tm, D), lambda i: (i, 0)),
            full(wglu_t), full(wpa_t), full(wout),
        ],
        out_specs=pl.BlockSpec((tm, D), lambda i: (i, 0)),
        out_shape=jax.ShapeDtypeStruct((T, D), F32),
        compiler_params=_cparams(("arbitrary",)),
        name="mixer_out",
    )(zt, zt, ys_t, att4, x2, wglu_t, wpa_t, wout)


def _memkv_kernel(mem_ref, g_ref, w_ref, kv_ref):
    kv_ref[...] = _dot(_rmsnorm(mem_ref[...], g_ref[...]).astype(BF16), w_ref[...]).astype(BF16)


def _memkv(mem2, g, w, rows):
    M, D = mem2.shape
    return pl.pallas_call(
        _memkv_kernel,
        grid=(M // rows,),
        in_specs=[pl.BlockSpec((rows, D), lambda i: (i, 0)), pl.BlockSpec((1, D), lambda i: (0, 0)),
                  pl.BlockSpec(w.shape, lambda i: (0, 0))],
        out_specs=pl.BlockSpec((rows, w.shape[1]), lambda i: (i, 0)),
        out_shape=jax.ShapeDtypeStruct((M, w.shape[1]), BF16),
        compiler_params=_cparams(("arbitrary",)),
        name="memkv",
    )(mem2, g, w)


def _xattn_kernel(h_ref, g_ref, wq_ref, kv_ref, wo_ref, o_ref):
    h = h_ref[...]
    q = _dot(_rmsnorm(h, g_ref[...]).astype(BF16), wq_ref[...]).astype(BF16)
    kv = kv_ref[...]
    xw = XATTN_HEADS * XATTN_HEAD_DIM
    scale = XATTN_HEAD_DIM ** -0.5
    outs = []
    for hd in range(XATTN_HEADS):
        sl = slice(hd * XATTN_HEAD_DIM, (hd + 1) * XATTN_HEAD_DIM)
        s = _dot_nt(q[:, sl], kv[:, sl]) * scale
        p = jnp.exp(s - jnp.max(s, axis=-1, keepdims=True))
        p = p / jnp.sum(p, axis=-1, keepdims=True)
        outs.append(_dot(p.astype(BF16), kv[:, xw + hd * XATTN_HEAD_DIM:xw + (hd + 1) * XATTN_HEAD_DIM]))
    o = jnp.concatenate(outs, axis=1).astype(BF16)
    o_ref[...] = h + _dot(o, wo_ref[...])


def _xattn(h1, g, wq, kv, wo, seq, mem_len, tm=512):
    T, D = h1.shape
    per_b = seq // tm
    return pl.pallas_call(
        _xattn_kernel,
        grid=(T // tm,),
        in_specs=[
            pl.BlockSpec((tm, D), lambda i: (i, 0)),
            pl.BlockSpec((1, D), lambda i: (0, 0)),
            pl.BlockSpec(wq.shape, lambda i: (0, 0)),
            pl.BlockSpec((mem_len, kv.shape[1]), lambda i: (i // per_b, 0)),
            pl.BlockSpec(wo.shape, lambda i: (0, 0)),
        ],
        out_specs=pl.BlockSpec((tm, D), lambda i: (i, 0)),
        out_shape=jax.ShapeDtypeStruct((T, D), F32),
        compiler_params=_cparams(("arbitrary",)),
        name="xattn",
    )(h1, g, wq, kv, wo)


def _router_kernel(h_ref, g_ref, wr_ref, br_ref, xn_ref, e_ref, r_ref, gate_ref, cnt_ref, carry_ref):
    i = pl.program_id(0)
    E = N_EXPERTS
    tm = h_ref.shape[0]

    @pl.when(i == 0)
    def _():
        carry_ref[...] = jnp.zeros_like(carry_ref)

    n = _rmsnorm(h_ref[...], g_ref[...])
    _to_row_tiles(xn_ref, n)
    na, nb_, nc = _split3(n)
    wa, wb, wc = _split3(wr_ref[...])
    logits = (_dot_nt(wa, na) + (_dot_nt(wa, nb_) + _dot_nt(wb, na))
              + (_dot_nt(wb, nb_) + _dot_nt(wa, nc) + _dot_nt(wc, na))) + br_ref[...]
    eidx = lax.broadcasted_iota(I32, (E, tm), 0)
    g = logits
    picks, vals = [], []
    for _ in range(EXPERT_TOPK):
        mx = jnp.max(g, axis=0, keepdims=True)
        first = jnp.min(jnp.where(g == mx, eidx, E), axis=0, keepdims=True)
        pick = eidx == first
        picks.append(pick)
        vals.append(mx)
        g = jnp.where(pick, -3e38, g)
    ex = [jnp.exp(v - vals[0]) for v in vals]
    tot = ex[0] + ex[1] + ex[2] + ex[3]
    sel = jnp.zeros((E, tm), F32)
    for pk in picks:
        sel = jnp.where(pk, 1.0, sel)
    before = (lax.broadcasted_iota(I32, (tm, tm), 0) < lax.broadcasted_iota(I32, (tm, tm), 1))
    prefix = _dot(sel.astype(BF16), jnp.where(before, 1.0, 0.0).astype(BF16))
    pos = prefix + carry_ref[:, 0:1]
    zi = jnp.zeros((8 - EXPERT_TOPK, tm), I32)
    e_rows = [jnp.sum(jnp.where(pk, eidx, 0), axis=0, keepdims=True) for pk in picks]
    r_rows = [jnp.sum(jnp.where(pk, pos, 0.0), axis=0, keepdims=True).astype(I32) for pk in picks]
    e_ref[...] = jnp.concatenate(e_rows + [zi], axis=0)
    r_ref[...] = jnp.concatenate(r_rows + [zi], axis=0)
    gate_rows = jnp.concatenate([x / tot for x in ex] + [jnp.zeros((LANES - EXPERT_TOPK, tm), F32)], axis=0)
    gate_ref[...] = gate_rows.T
    carry_ref[...] = carry_ref[...] + jnp.sum(sel, axis=1, keepdims=True)
    cnt_ref[...] = carry_ref[...]


def _router(h2, g, wr_t, br, tm=256):
    T, D = h2.shape
    E = N_EXPERTS
    return pl.pallas_call(
        _router_kernel,
        grid=(T // tm,),
        in_specs=[pl.BlockSpec((tm, D), lambda i: (i, 0)), pl.BlockSpec((1, D), lambda i: (0, 0)),
                  pl.BlockSpec((E, D), lambda i: (0, 0)), pl.BlockSpec((E, 1), lambda i: (0, 0))],
        out_specs=[pl.BlockSpec((tm * ROW_TILE, LANES), lambda i: (i, 0)),
                   pl.BlockSpec((8, tm), lambda i: (0, i)), pl.BlockSpec((8, tm), lambda i: (0, i)),
                   pl.BlockSpec((tm, LANES), lambda i: (i, 0)),
                   pl.BlockSpec((E, LANES), lambda i: (0, 0))],
        out_shape=[jax.ShapeDtypeStruct((T * ROW_TILE, LANES), F32),
                   jax.ShapeDtypeStruct((8, T), I32), jax.ShapeDtypeStruct((8, T), I32),
                   jax.ShapeDtypeStruct((T, LANES), F32),
                   jax.ShapeDtypeStruct((E, LANES), F32)],
        scratch_shapes=[pltpu.VMEM((E, LANES), F32)],
        compiler_params=_cparams(("arbitrary",)),
        name="router",
    )(h2, g, wr_t, br)


def _positions_kernel(ps_ref, e_ref, r_ref, pos_ref):
    e = e_ref[...]
    pos = r_ref[...]
    for x in range(N_EXPERTS):
        pos = pos + jnp.where(e == x, ps_ref[x], 0)
    for c in range(pos_ref.shape[0]):
        pos_ref[c] = pos[:, c * LANES:(c + 1) * LANES]


def _positions(pstart, e8, r8, tm=2048):
    T = e8.shape[1]
    grid_spec = pltpu.PrefetchScalarGridSpec(
        num_scalar_prefetch=1,
        grid=(T // tm,),
        in_specs=[pl.BlockSpec((8, tm), lambda i, ps: (0, i)), pl.BlockSpec((8, tm), lambda i, ps: (0, i))],
        out_specs=pl.BlockSpec((tm // LANES, 8, LANES), lambda i, ps: (i, 0, 0)),
    )
    return pl.pallas_call(
        _positions_kernel, grid_spec=grid_spec, out_shape=jax.ShapeDtypeStruct((T // LANES, 8, LANES), I32),
        compiler_params=_cparams(("arbitrary",)), name="positions",
    )(pstart, e8, r8)


def _row_dmas(tm, make_copy, wait):
    for c in range(tm // LANES):
        for k0 in range(0, EXPERT_TOPK, 2):
            def body(lane, carry, c=c, k0=k0):
                for k in (k0, k0 + 1):
                    cp = make_copy(c, k, lane)
                    cp.wait() if wait else cp.start(priority=k % 2)
                return carry
            lax.fori_loop(0, LANES, body, 0, unroll=4)


def _dispatch_kernel(zr_ref, pos_ref, xn_ref, xs_ref, zbuf, sem, zsem):
    tm = xn_ref.shape[0] // ROW_TILE

    @pl.when(pl.program_id(0) == 0)
    def _():
        zbuf[...] = jnp.zeros_like(zbuf)

        def zero_copy(e):
            first = pl.multiple_of(zr_ref[e] * ROW_TILE, MOE_BLOCK * ROW_TILE)
            return pltpu.make_async_copy(zbuf, xs_ref.at[pl.ds(first, MOE_BLOCK * ROW_TILE)], zsem)

        for e in range(2 * N_EXPERTS):
            @pl.when(zr_ref[e] >= 0)
            def _():
                zero_copy(e).start()
        for e in range(2 * N_EXPERTS):
            @pl.when(zr_ref[e] >= 0)
            def _():
                zero_copy(e).wait()

    row_copy = lambda c, k, lane: pltpu.make_async_copy(
        _tile(xn_ref, c * LANES + lane), _tile(xs_ref, _pos(pos_ref, c, k, lane)), sem)
    _row_dmas(tm, row_copy, wait=False)
    _row_dmas(tm, row_copy, wait=True)


def _dispatch(zrow, pos3, xn, rows_total, tm=1024):
    T = xn.shape[0] // ROW_TILE
    grid_spec = pltpu.PrefetchScalarGridSpec(
        num_scalar_prefetch=1,
        grid=(T // tm,),
        in_specs=[pl.BlockSpec((tm * 8,), lambda i, zr: (i,), memory_space=pltpu.SMEM),
                  pl.BlockSpec((tm * ROW_TILE, LANES), lambda i, zr: (i, 0))],
        out_specs=pl.BlockSpec(memory_space=pl.ANY),
        scratch_shapes=[pltpu.VMEM((MOE_BLOCK * ROW_TILE, LANES), F32), pltpu.SemaphoreType.DMA(()),
                        pltpu.SemaphoreType.DMA(())],
    )
    return pl.pallas_call(
        _dispatch_kernel,
        grid_spec=grid_spec,
        out_shape=jax.ShapeDtypeStruct((rows_total * ROW_TILE, LANES), F32),
        compiler_params=_cparams(("arbitrary",)),
        name="dispatch",
    )(zrow, pos3, xn)


def _experts_kernel(be_ref, nu_ref, xs_ref, wgu_ref, bgu_ref, wdn_ref, bdn_ref, ys_ref, wgu_bf, wdn_bf):
    i = pl.program_id(0)
    dff = wdn_ref.shape[1]

    @pl.when(i < nu_ref[0])
    def _():
        prev = be_ref[jnp.maximum(i - 1, 0)]

        @pl.when((i == 0) | (be_ref[i] != prev))
        def _():
            wgu_bf[...] = wgu_ref[0].astype(BF16)
            wdn_bf[...] = wdn_ref[0].astype(BF16)

        gu = _dot(_from_row_tiles(xs_ref, MOE_BLOCK).astype(BF16), wgu_bf[...]) + bgu_ref[0]
        gate = jnp.minimum(gu[:, :dff], SWIGLU_LIMIT)
        up = jnp.clip(gu[:, dff:], -SWIGLU_LIMIT, SWIGLU_LIMIT)
        hid = (up + 1.0) * gate * jax.nn.sigmoid(SWIGLU_ALPHA * gate)
        _to_row_tiles(ys_ref, _dot(hid.astype(BF16), wdn_bf[...]) + bdn_ref[0])

    @pl.when(i >= nu_ref[0])
    def _():
        ys_ref[...] = jnp.zeros_like(ys_ref)


def _experts(blk_expert, nused, xs, wgu, bgu, wdn, bdn):
    rows = xs.shape[0] // ROW_TILE
    E, D, F2 = wgu.shape
    dff = wdn.shape[1]
    nblk = rows // MOE_BLOCK
    grid_spec = pltpu.PrefetchScalarGridSpec(
        num_scalar_prefetch=2,
        grid=(nblk,),
        in_specs=[pl.BlockSpec((MOE_BLOCK * ROW_TILE, LANES), lambda i, be, nu: (jnp.minimum(i, nu[0] - 1), 0)),
                  pl.BlockSpec((1, D, F2), lambda i, be, nu: (be[i], 0, 0)),
                  pl.BlockSpec((1, 1, F2), lambda i, be, nu: (be[i], 0, 0)),
                  pl.BlockSpec((1, dff, D), lambda i, be, nu: (be[i], 0, 0)),
                  pl.BlockSpec((1, 1, D), lambda i, be, nu: (be[i], 0, 0))],
        out_specs=pl.BlockSpec((MOE_BLOCK * ROW_TILE, LANES), lambda i, be, nu: (i, 0)),
        scratch_shapes=[pltpu.VMEM((D, F2), BF16), pltpu.VMEM((dff, D), BF16)],
    )
    return pl.pallas_call(
        _experts_kernel,
        grid_spec=grid_spec,
        out_shape=jax.ShapeDtypeStruct((rows * ROW_TILE, LANES), F32),
        compiler_params=_cparams(("arbitrary",)),
        name="experts",
    )(blk_expert, nused, xs, wgu, bgu, wdn, bdn)


def _combine_kernel(pos_ref, nxt_ref, h_ref, gate_ref, gf_ref, ys_ref, o_ref, buf, sem):
    i = pl.program_id(0)
    tm = h_ref.shape[0]
    slot = i % 2

    def gather(p_ref, slot):
        first = slot * (EXPERT_TOPK * tm)
        return lambda c, k, lane: pltpu.make_async_copy(
            _tile(ys_ref, _pos(p_ref, c, k, lane)), _tile(buf, first + k * tm + c * LANES + lane), sem.at[slot])

    @pl.when(i == 0)
    def _():
        _row_dmas(tm, gather(pos_ref, slot), wait=False)

    @pl.when(i + 1 < pl.num_programs(0))
    def _():
        _row_dmas(tm, gather(nxt_ref, 1 - slot), wait=False)

    _row_dmas(tm, gather(pos_ref, slot), wait=True)
    gates = gate_ref[...]
    acc = h_ref[...]
    for k in range(EXPERT_TOPK):
        acc = acc + gates[:, k:k + 1] * _from_row_tiles(buf, tm, first=slot * (EXPERT_TOPK * tm) + k * tm)
    o_ref[...] = _rmsnorm(acc, gf_ref[...])


def _combine(pos3, h2, gates, gf, ys, tm=256):
    T, D = h2.shape
    return pl.pallas_call(
        _combine_kernel,
        grid=(T // tm,),
        in_specs=[pl.BlockSpec((tm * 8,), lambda i: (i,), memory_space=pltpu.SMEM),
                  pl.BlockSpec((tm * 8,), lambda i: (jnp.minimum(i + 1, T // tm - 1),), memory_space=pltpu.SMEM),
                  pl.BlockSpec((tm, D), lambda i: (i, 0)),
                  pl.BlockSpec((tm, LANES), lambda i: (i, 0)),
                  pl.BlockSpec((1, D), lambda i: (0, 0)),
                  pl.BlockSpec(memory_space=pl.ANY)],
        out_specs=pl.BlockSpec((tm, D), lambda i: (i, 0)),
        out_shape=jax.ShapeDtypeStruct((T, D), F32),
        scratch_shapes=[pltpu.VMEM((2 * EXPERT_TOPK * tm * ROW_TILE, LANES), F32), pltpu.SemaphoreType.DMA((2,))],
        compiler_params=_cparams(("arbitrary",)),
        name="combine",
    )(pos3, pos3, h2, gates, gf, ys)


def _pad_lanes(a, width=LANES, value=0.0):
    pad = [(0, 0)] * (a.ndim - 1) + [(0, width - a.shape[-1])]
    return jnp.pad(a, pad, constant_values=value)


def _layer(h2d, mem2d, batch, seq, mem_len, g_mix, w_in, a_re, a_im, log_dt, b_re, b_im, c_re, c_im, d_skip,
           w_glu, w_proj_a, w_out, g_xattn, g_mem, w_xq, w_xkv, w_xo, g_moe, w_router, b_router,
           w_gu, b_gu, w_dn, b_dn, g_final):
    T, D = h2d.shape
    aw = ATT_HEADS * ATT_HEAD_DIM
    G = a_re.shape[0]
    sw = G * SSM_GROUP

    wq, wk, wv, wu, wga, wgb = jnp.split(w_in, [aw, 2 * aw, 3 * aw, 3 * aw + sw, 3 * aw + sw + D], axis=1)
    wg_t = jnp.concatenate([wga, wgb, wu], axis=1).T.astype(BF16)
    wqv_t = jnp.concatenate([wq * (ATT_HEAD_DIM ** -0.5), wv], axis=1).T.astype(BF16)
    wk_pad = _pad_lanes(wk.reshape(D, ATT_HEADS, ATT_HEAD_DIM)).reshape(D, ATT_HEADS * LANES).astype(BF16)

    zt, q4, v4, kpad, km = _in_proj(h2d, g_mix.reshape(1, D), wg_t, wqv_t, wk_pad)
    nb = seq // MOBA_BLOCK
    att4 = _moba(q4, v4, kpad.reshape(T // MOBA_BLOCK, MOBA_BLOCK, ATT_HEADS * LANES),
                 km.reshape(batch, nb, ATT_HEADS * LANES), batch)

    row = lambda a, v=0.0: _pad_lanes(a, value=v).reshape(G, 1, LANES)
    col = lambda a, v=0.0: _pad_lanes(a, value=v).reshape(G, LANES, 1)
    ssm_params = (
        row(a_re, -1.0), row(a_im), log_dt.reshape(G, 1, 1), col(a_re, -1.0), col(a_im),
        _pad_lanes(jnp.swapaxes(b_re, 1, 2)), _pad_lanes(jnp.swapaxes(b_im, 1, 2)),
        _pad_lanes(c_re), _pad_lanes(c_im),
        _pad_lanes(c_re)[..., None], _pad_lanes(c_im)[..., None],
        d_skip.reshape(G, SSM_GROUP, 1, 1),
    )
    ys3 = _ssm(zt.reshape(zt.shape[0], T // SSM_CHUNK, SSM_CHUNK), 2 * D, ssm_params, batch)
    h1 = _mixer_out(zt, ys3.reshape(sw, T), att4, h2d, w_glu.T.astype(BF16), w_proj_a.T.astype(BF16),
                    w_out.astype(BF16))

    kv = _memkv(mem2d, g_mem.reshape(1, D), w_xkv.astype(BF16), mem_len)
    h2 = _xattn(h1, g_xattn.reshape(1, D), w_xq.astype(BF16), kv, w_xo.astype(BF16), seq, mem_len)

    xn, e8, r8, gates, cnt = _router(h2, g_moe.reshape(1, D), w_router.T, b_router.reshape(N_EXPERTS, 1))
    counts = cnt[:, 0].astype(I32)
    nblk_e = (counts + MOE_BLOCK - 1) // MOE_BLOCK
    bends = jnp.cumsum(nblk_e)
    pstart = ((bends - nblk_e) * MOE_BLOCK).astype(I32)
    nblk = (T * EXPERT_TOPK) // MOE_BLOCK + N_EXPERTS
    blk_expert = jnp.minimum(jnp.sum(bends[None, :] <= jnp.arange(nblk, dtype=I32)[:, None], axis=1),
                             N_EXPERTS - 1).astype(I32)
    nused = bends[-1:].astype(I32)
    blk_expert = jnp.where(jnp.arange(nblk) < nused[0], blk_expert, blk_expert[jnp.maximum(nused[0] - 1, 0)])

    tail = nblk - 1 - jnp.arange(N_EXPERTS, dtype=I32)
    zrow = jnp.concatenate([jnp.where(nblk_e > 0, (bends - 1) * MOE_BLOCK, -1),
                            jnp.where(tail >= nused[0], tail * MOE_BLOCK, -1)]).astype(I32)
    pos3 = _positions(pstart, e8, r8).reshape(-1)
    xs = _dispatch(zrow, pos3, xn, nblk * MOE_BLOCK)
    ys = _experts(blk_expert, nused, xs, w_gu, b_gu.reshape(N_EXPERTS, 1, -1), w_dn, b_dn.reshape(N_EXPERTS, 1, -1))
    return _combine(pos3, h2, gates, g_final.reshape(1, D), ys)


def kernel(x, mem, g_mix, w_in, a_re, a_im, log_dt, b_re, b_im, c_re, c_im, d_skip, w_glu, w_proj_a, w_out,
           g_xattn, g_mem, w_xq, w_xkv, w_xo, g_moe, w_router, b_router, w_gu, b_gu, w_dn, b_dn, g_final):
    B, S, D = x.shape
    M = mem.shape[1]
    assert g_mix.shape[0] == 1, "single layer"
    out = _layer(x.reshape(B * S, D), mem.reshape(B * M, D), B, S, M,
                 g_mix[0], w_in[0], a_re[0], a_im[0], log_dt[0], b_re[0], b_im[0], c_re[0], c_im[0], d_skip[0],
                 w_glu[0], w_proj_a[0], w_out[0], g_xattn[0], g_mem[0], w_xq[0], w_xkv[0], w_xo[0],
                 g_moe[0], w_router[0], b_router[0], w_gu[0], b_gu[0], w_dn[0], b_dn[0], g_final)
    return out.reshape(B, S, D)
```

```python
import functools
import math

import jax
import jax.numpy as jnp
from jax import lax
from jax.experimental import pallas as pl
from jax.experimental.pallas import tpu as pltpu

F32 = jnp.float32
BF16 = jnp.bfloat16
I32 = jnp.int32

RMS_EPS = 1e-6
NEG_INF = -1e30
ATT_HEADS = 8
ATT_HEAD_DIM = 64
MOBA_BLOCK = 256
MOBA_TOPK = 3
MOBA_HEADS_PER_STEP = 4
SSM_GROUP = 16
SSM_STATE = 64
SSM_CHUNK = 128
XATTN_HEADS = 4
XATTN_HEAD_DIM = 128
N_EXPERTS = 32
EXPERT_TOPK = 4
SWIGLU_LIMIT = 7.0
SWIGLU_ALPHA = 1.702
MOE_BLOCK = 512
LANES = 128
VMEM_LIMIT = 56 * 1024 * 1024


def _cparams(sem):
    return pltpu.CompilerParams(dimension_semantics=sem, vmem_limit_bytes=VMEM_LIMIT)


def _rmsnorm(x, g):
    return x * lax.rsqrt(jnp.mean(x * x, axis=-1, keepdims=True) + RMS_EPS) * g


def _dot(a, b):
    return jnp.dot(a, b, preferred_element_type=F32)


def _dot_nt(a, b):
    return lax.dot_general(a, b, (((1,), (1,)), ((), ())), preferred_element_type=F32)


def _dot_tn(a, b):
    return lax.dot_general(a, b, (((0,), (0,)), ((), ())), preferred_element_type=F32)


def _split3(x):
    a = x.astype(BF16)
    r = x - a.astype(F32)
    b = r.astype(BF16)
    c = (r - b.astype(F32)).astype(BF16)
    return a, b, c


ROW_TILE = 8


def _to_row_tiles(ref, x, first=0):
    for s in range(ROW_TILE):
        ref[pl.ds(first * ROW_TILE + s, x.shape[0], stride=ROW_TILE), :] = x[:, s * LANES:(s + 1) * LANES]


def _from_row_tiles(ref, rows, first=0):
    return jnp.concatenate(
        [ref[pl.ds(first * ROW_TILE + s, rows, stride=ROW_TILE), :] for s in range(ROW_TILE)], axis=1)


def _pos(pos_ref, c, k, lane):
    return pos_ref[(c * 8 + k) * LANES + lane]


def _tile(ref, r):
    return ref.at[pl.ds(pl.multiple_of(r * ROW_TILE, ROW_TILE), ROW_TILE)]


def _inproj_kernel(x_ref, g_ref, wg_ref, wqv_ref, wk_ref, zt_ref, q_ref, v_ref, k_ref, km_ref):
    n = _rmsnorm(x_ref[...], g_ref[...]).astype(BF16)
    tm = n.shape[0]
    nblk = tm // MOBA_BLOCK
    rows = 512
    for c in range(wg_ref.shape[0] // rows):
        zt_ref[c * rows:(c + 1) * rows, :] = _dot_nt(wg_ref[c * rows:(c + 1) * rows, :], n).astype(BF16)
    for c, dst in ((0, q_ref), (1, v_ref)):
        r = _dot_nt(wqv_ref[c * rows:(c + 1) * rows, :], n).astype(BF16)
        for h in range(ATT_HEADS):
            for j in range(nblk):
                dst[h, j] = r[h * ATT_HEAD_DIM:(h + 1) * ATT_HEAD_DIM, j * MOBA_BLOCK:(j + 1) * MOBA_BLOCK]
    k = _dot(n, wk_ref[...])
    lane = lax.broadcasted_iota(I32, (1, k.shape[1]), 1) & (LANES - 1)
    k = k + jnp.where(lane == ATT_HEAD_DIM, 1.0, 0.0)
    k_ref[...] = k.astype(BF16)
    km_ref[0] = jnp.mean(k.reshape(nblk, MOBA_BLOCK, k.shape[1]), axis=1)


def _in_proj(x2, g, wg_t, wqv_t, wk_pad, tm=512):
    T, D = x2.shape
    nblk = tm // MOBA_BLOCK
    ng = wg_t.shape[0]
    kw = wk_pad.shape[1]
    hd = (ATT_HEADS, T // MOBA_BLOCK, ATT_HEAD_DIM, MOBA_BLOCK)
    return pl.pallas_call(
        _inproj_kernel,
        grid=(T // tm,),
        in_specs=[
            pl.BlockSpec((tm, D), lambda i: (i, 0)),
            pl.BlockSpec((1, D), lambda i: (0, 0)),
            pl.BlockSpec((ng, D), lambda i: (0, 0)),
            pl.BlockSpec(wqv_t.shape, lambda i: (0, 0)),
            pl.BlockSpec((D, kw), lambda i: (0, 0)),
        ],
        out_specs=[
            pl.BlockSpec((ng, tm), lambda i: (0, i)),
            pl.BlockSpec((ATT_HEADS, nblk, ATT_HEAD_DIM, MOBA_BLOCK), lambda i: (0, i, 0, 0)),
            pl.BlockSpec((ATT_HEADS, nblk, ATT_HEAD_DIM, MOBA_BLOCK), lambda i: (0, i, 0, 0)),
            pl.BlockSpec((tm, kw), lambda i: (i, 0)),
            pl.BlockSpec((1, nblk, kw), lambda i: (i, 0, 0)),
        ],
        out_shape=[
            jax.ShapeDtypeStruct((ng, T), BF16),
            jax.ShapeDtypeStruct(hd, BF16),
            jax.ShapeDtypeStruct(hd, BF16),
            jax.ShapeDtypeStruct((T, kw), BF16),
            jax.ShapeDtypeStruct((T // tm, nblk, kw), F32),
        ],
        compiler_params=_cparams(("arbitrary",)),
        name="in_proj",
    )(x2, g, wg_t, wqv_t, wk_pad)


def _moba_kernel(q_ref, v_ref, k_ref, km_ref, o_ref, sel_ref, acc_ref, sa_ref, sb_ref, pb_ref, *, nb, hb):
    blk = MOBA_BLOCK
    heads = range(hb)
    lanes = lambda h: slice(h * LANES, (h + 1) * LANES)
    kms = [_split3(km_ref[0][:, lanes(h)]) for h in heads]
    zpad = jnp.zeros((LANES - ATT_HEAD_DIM, blk), BF16)
    bidx = lax.broadcasted_iota(I32, (nb, blk), 0)
    causal = lax.broadcasted_iota(I32, (blk, blk), 0) <= lax.broadcasted_iota(I32, (blk, blk), 1)
    topk = min(MOBA_TOPK, nb)

    row16 = lax.broadcasted_iota(I32, (16, blk), 0) == 0
    ones16 = jnp.where(row16, 1.0, 0.0).astype(BF16)
    zpad48 = jnp.zeros((LANES - ATT_HEAD_DIM - 16, blk), BF16)

    def qblock(i, carry):
        elig = bidx < i
        q64 = [q_ref[h, i] for h in heads]
        for h in heads:
            q = jnp.concatenate([q64[h], zpad], axis=0)
            gate = (_dot(kms[h][0], q) + _dot(kms[h][1], q)) + _dot(kms[h][2], q)
            g = jnp.where(elig, gate, NEG_INF)
            sel = jnp.zeros((nb, blk), F32)
            for _ in range(topk):
                mx = jnp.max(g, axis=0, keepdims=True)
                first = jnp.min(jnp.where(g == mx, bidx, nb), axis=0, keepdims=True)
                pick = bidx == first
                sel = jnp.where(pick, 1.0, sel)
                g = jnp.where(pick, -3e38, g)
            sel_ref[h] = jnp.where(elig, sel, 0.0)
            acc_ref[h] = jnp.zeros((ATT_HEAD_DIM + 16, blk), F32)
            pb_ref[h] = jnp.zeros((blk, blk), BF16)

        def scores(h, t):
            bias = (sel_ref[h, pl.ds(t, 1), :] - 1.0) * 1e30
            q = jnp.concatenate([q64[h], jnp.where(row16, bias, 0.0).astype(BF16), zpad48], axis=0)
            return _dot(k_ref[t][:, lanes(h)], q)

        def values(h, t, p):
            return _dot(jnp.concatenate([v_ref[h, t], ones16], axis=0), p)

        def softmax(s, m):
            m_new = jnp.maximum(m, jnp.max(s, axis=0, keepdims=True))
            return m_new, jnp.exp(m - m_new), jnp.exp(s - m_new).astype(BF16)

        for h in heads:
            sa_ref[h] = scores(h, 0)

        def kvpair(u, c):
            ta, tb, tn = 2 * u, 2 * u + 1, jnp.minimum(2 * u + 2, nb - 1)
            pv = [values(h, jnp.maximum(ta - 1, 0), pb_ref[h]) for h in heads]
            sb = [scores(h, tb) for h in heads]
            for h in heads:
                sb_ref[h] = sb[h]
                acc_ref[h] = acc_ref[h] * c[2 * h + 1] + pv[h]
            sm = [softmax(sa_ref[h], c[2 * h]) for h in heads]
            pv = [values(h, ta, sm[h][2]) for h in heads]
            sa = [scores(h, tn) for h in heads]
            out = []
            for h in heads:
                sa_ref[h] = sa[h]
                acc_ref[h] = acc_ref[h] * sm[h][1] + pv[h]
                m_new, alpha, p = softmax(sb_ref[h], sm[h][0])
                pb_ref[h] = p
                out += [m_new, alpha]
            return tuple(out)

        m0 = jnp.full((1, blk), -1e29, F32)
        npair = (i + 1) // 2
        c = lax.fori_loop(0, npair, kvpair, (m0, jnp.ones((1, blk), F32)) * hb)
        pv = [values(h, jnp.maximum(2 * npair - 1, 0), pb_ref[h]) for h in heads]
        kd = k_ref[i]
        own = [_dot(kd[:, lanes(h)], jnp.concatenate([q64[h], zpad], axis=0)) for h in heads]
        sm = [softmax(jnp.where(causal, own[h], NEG_INF), c[2 * h]) for h in heads]
        for h in heads:
            res = (acc_ref[h] * c[2 * h + 1] + pv[h]) * sm[h][1] + values(h, i, sm[h][2])
            o_ref[h, i] = (res[:ATT_HEAD_DIM] / res[ATT_HEAD_DIM:ATT_HEAD_DIM + 1]).astype(BF16)
        return carry

    lax.fori_loop(0, nb, qblock, 0)


def _moba(q4, v4, k3, km3, batch, hb=MOBA_HEADS_PER_STEP):
    H, nblk_total, dh, blk = q4.shape
    nb = nblk_total // batch
    return pl.pallas_call(
        functools.partial(_moba_kernel, nb=nb, hb=hb),
        grid=(batch, H // hb),
        in_specs=[
            pl.BlockSpec((hb, nb, dh, blk), lambda b, h: (h, b, 0, 0)),
            pl.BlockSpec((hb, nb, dh, blk), lambda b, h: (h, b, 0, 0)),
            pl.BlockSpec((nb, blk, hb * LANES), lambda b, h: (b, 0, h)),
            pl.BlockSpec((1, nb, hb * LANES), lambda b, h: (b, 0, h)),
        ],
        out_specs=pl.BlockSpec((hb, nb, dh, blk), lambda b, h: (h, b, 0, 0)),
        out_shape=jax.ShapeDtypeStruct(q4.shape, BF16),
        scratch_shapes=[pltpu.VMEM((hb, nb, blk), F32),
                        pltpu.VMEM((hb, dh + 16, blk), F32),
                        pltpu.VMEM((hb, blk, blk), F32),
                        pltpu.VMEM((hb, blk, blk), F32),
                        pltpu.VMEM((hb, blk, blk), BF16)],
        compiler_params=_cparams(("arbitrary", "arbitrary")),
        name="moba",
    )(q4, v4, k3, km3)


def _ssm_kernel(u_ref, ar_ref, ai_ref, ldt_ref, arc_ref, aic_ref, btr_ref, bti_ref, cr_ref, ci_ref,
                ctr_ref, cti_ref, d_ref, y_ref,
                u2_ref, m_ref, pr_ref, pi_ref, qr_ref, qi_ref, kv_ref, sr_ref, si_ref, xr_ref, xi_ref,
                *, nbatch):
    L = SSM_CHUNK
    C = SSM_GROUP
    N = u_ref.shape[1]
    nchunk = N // nbatch
    hi = lax.Precision.HIGHEST

    dt = jnp.exp(ldt_ref[0])
    are = jnp.minimum(ar_ref[0], -1e-4)
    aim = ai_ref[0]
    ea, th = are * dt, aim * dt
    mag = jnp.exp(ea)
    lbr, lbi = mag * jnp.cos(th), mag * jnp.sin(th)
    den = are * are + aim * aim
    cfr = ((lbr - 1.0) * are + lbi * aim) / den
    cfi = (lbi * are - (lbr - 1.0) * aim) / den
    btr, bti = btr_ref[0], bti_ref[0]
    bbr = cfr * btr - cfi * bti
    bbi = cfr * bti + cfi * btr
    crr, cri = cr_ref[0], ci_ref[0]

    eac = jnp.minimum(arc_ref[0], -1e-4) * dt
    thc = aic_ref[0] * dt
    tau = lax.broadcasted_iota(I32, (1, L), 1).astype(F32)
    m0 = jnp.exp(eac * tau)
    pw0r, pw0i = m0 * jnp.cos(thc * tau), m0 * jnp.sin(thc * tau)
    m1 = jnp.exp(eac * (tau + 1.0))
    pw1r, pw1i = m1 * jnp.cos(thc * (tau + 1.0)), m1 * jnp.sin(thc * (tau + 1.0))
    back = (L - 1.0) - lax.broadcasted_iota(I32, (L, 1), 0).astype(F32)
    mb = jnp.exp(ea * back)
    pbr, pbi = mb * jnp.cos(th * back), mb * jnp.sin(th * back)
    ml = jnp.exp(ea * float(L))
    alr, ali = ml * jnp.cos(th * float(L)), ml * jnp.sin(th * float(L))

    upper = lax.broadcasted_iota(I32, (L, L), 1) >= lax.broadcasted_iota(I32, (L, L), 0)
    for ci in range(C):
        cbr = bbr[ci:ci + 1, :] * crr - bbi[ci:ci + 1, :] * cri
        cbi = bbr[ci:ci + 1, :] * cri + bbi[ci:ci + 1, :] * crr
        kv_ref[ci * C:(ci + 1) * C, :] = (jnp.dot(cbr, pw0r, precision=hi, preferred_element_type=F32)
                                          - jnp.dot(cbi, pw0i, precision=hi, preferred_element_type=F32))
        pr_ref[ci * L:(ci + 1) * L, :] = (pbr * bbr[ci:ci + 1, :] - pbi * bbi[ci:ci + 1, :]).astype(BF16)
        pi_ref[ci * L:(ci + 1) * L, :] = (pbr * bbi[ci:ci + 1, :] + pbi * bbr[ci:ci + 1, :]).astype(BF16)
        u2_ref[:, ci * L:(ci + 1) * L] = u_ref[ci]
    for co in range(C):
        cc_r, cc_i = ctr_ref[0, co], cti_ref[0, co]
        qr_ref[:, co * L:(co + 1) * L] = (cc_r * pw1r - cc_i * pw1i).astype(BF16)
        qi_ref[:, co * L:(co + 1) * L] = (-(cc_r * pw1i + cc_i * pw1r)).astype(BF16)

    wide = 4 * L
    nwide = C * L // wide

    def toeplitz(n0):
        for ci in range(C):
            for co in range(n0 * (wide // L), (n0 + 1) * (wide // L)):
                taps = kv_ref[ci * C + co:ci * C + co + 1, :]
                t = pltpu.roll(jnp.broadcast_to(taps, (L, L)), 0, 1, stride=1, stride_axis=0)
                m_ref[ci * L:(ci + 1) * L, co * L:(co + 1) * L] = jnp.where(upper, t, 0.0).astype(BF16)

    u2 = u2_ref[...]
    sr_ref[...] = _dot(u2, pr_ref[...]).reshape(nbatch, nchunk, LANES)
    si_ref[...] = _dot(u2, pi_ref[...]).reshape(nbatch, nchunk, LANES)
    toeplitz(0)

    def chunk_scan(c, carry):
        xr, xi = carry
        xr_ref[:, pl.ds(c, 1), :] = xr
        xi_ref[:, pl.ds(c, 1), :] = xi
        nr = alr * xr - ali * xi + sr_ref[:, pl.ds(c, 1), :]
        ni = alr * xi + ali * xr + si_ref[:, pl.ds(c, 1), :]
        return nr, ni

    zero = jnp.zeros((nbatch, 1, LANES), F32)
    lax.fori_loop(0, nchunk, chunk_scan, (zero, zero))
    xpr = xr_ref[...].reshape(N, LANES).astype(BF16)
    xpi = xi_ref[...].reshape(N, LANES).astype(BF16)

    for n0 in range(nwide):
        y = (_dot(u2, m_ref[:, n0 * wide:(n0 + 1) * wide])
             + _dot(xpr, qr_ref[:, n0 * wide:(n0 + 1) * wide])
             + _dot(xpi, qi_ref[:, n0 * wide:(n0 + 1) * wide]))
        if n0 + 1 < nwide:
            toeplitz(n0 + 1)
        for cc in range(wide // L):
            co = n0 * (wide // L) + cc
            y_ref[co] = (y[:, cc * L:(cc + 1) * L] + d_ref[0, co] * u_ref[co].astype(F32)).astype(BF16)


def _ssm(zt3, row0, params, nbatch):
    (ar, ai, ldt, arc, aic, btr, bti, cr, ci, ctr, cti, d) = params
    G = ar.shape[0]
    C, L = SSM_GROUP, SSM_CHUNK
    N = zt3.shape[1]
    g3 = lambda g: (g, 0, 0)
    g4 = lambda g: (g, 0, 0, 0)
    return pl.pallas_call(
        functools.partial(_ssm_kernel, nbatch=nbatch),
        grid=(G,),
        in_specs=[
            pl.BlockSpec((C, N, L), lambda g: (row0 // C + g, 0, 0)),
            pl.BlockSpec((1, 1, LANES), g3), pl.BlockSpec((1, 1, LANES), g3), pl.BlockSpec((1, 1, 1), g3),
            pl.BlockSpec((1, LANES, 1), g3), pl.BlockSpec((1, LANES, 1), g3),
            pl.BlockSpec((1, C, LANES), g3), pl.BlockSpec((1, C, LANES), g3),
            pl.BlockSpec((1, C, LANES), g3), pl.BlockSpec((1, C, LANES), g3),
            pl.BlockSpec((1, C, LANES, 1), g4), pl.BlockSpec((1, C, LANES, 1), g4),
            pl.BlockSpec((1, C, 1, 1), g4),
        ],
        out_specs=pl.BlockSpec((C, N, L), g3),
        out_shape=jax.ShapeDtypeStruct((G * C, N, L), BF16),
        scratch_shapes=[
            pltpu.VMEM((N, C * L), BF16),
            pltpu.VMEM((C * L, C * L), BF16),
            pltpu.VMEM((C * L, LANES), BF16), pltpu.VMEM((C * L, LANES), BF16),
            pltpu.VMEM((LANES, C * L), BF16), pltpu.VMEM((LANES, C * L), BF16),
            pltpu.VMEM((C * C, L), F32),
            pltpu.VMEM((nbatch, N // nbatch, LANES), F32), pltpu.VMEM((nbatch, N // nbatch, LANES), F32),
            pltpu.VMEM((nbatch, N // nbatch, LANES), F32), pltpu.VMEM((nbatch, N // nbatch, LANES), F32),
        ],
        compiler_params=_cparams(("arbitrary",)),
        name="ssm",
    )(zt3, ar, ai, ldt, arc, aic, btr, bti, cr, ci, ctr, cti, d)


def _mixer_out_kernel(ga_ref, gb_ref, ys_ref, at_ref, x_ref, wglu_ref, wpa_ref, wout_ref, h_ref):
    dm = ga_ref.shape[0]
    nblk = at_ref.shape[1]
    att = jnp.concatenate(
        [jnp.concatenate([at_ref[h, j] for j in range(nblk)], axis=1) for h in range(ATT_HEADS)], axis=0)
    ya = _dot(wpa_ref[...], att)
    gl = jax.nn.gelu(ys_ref[...].astype(F32)).astype(BF16)
    zb = _dot(wglu_ref[...], gl)
    yb = zb[:dm] * jax.nn.sigmoid(zb[dm:])
    merged = jax.nn.sigmoid(ga_ref[...].astype(F32)) * ya + jax.nn.sigmoid(gb_ref[...].astype(F32)) * yb
    h_ref[...] = x_ref[...] + _dot_tn(merged.astype(BF16), wout_ref[...])


def _mixer_out(zt, ys_t, att4, x2, wglu_t, wpa_t, wout, tm=512):
    T, D = x2.shape
    nblk = tm // MOBA_BLOCK
    sw = ys_t.shape[0]
    full = lambda a: pl.BlockSpec(a.shape, lambda i: (0,) * a.ndim)
    return pl.pallas_call(
        _mixer_out_kernel,
        grid=(T // tm,),
        in_specs=[
            pl.BlockSpec((D, tm), lambda i: (0, i)),
            pl.BlockSpec((D, tm), lambda i: (1, i)),
            pl.BlockSpec((sw, tm), lambda i: (0, i)),
            pl.BlockSpec((ATT_HEADS, nblk, ATT_HEAD_DIM, MOBA_BLOCK), lambda i: (0, i, 0, 0)),
            pl.BlockSpec((tm, D), lambda i: (i, 0)),
            full(wglu_t), full(wpa_t), full(wout),
        ],
        out_specs=pl.BlockSpec((tm, D), lambda i: (i, 0)),
        out_shape=jax.ShapeDtypeStruct((T, D), F32),
        compiler_params=_cparams(("arbitrary",)),
        name="mixer_out",
    )(zt, zt, ys_t, att4, x2, wglu_t, wpa_t, wout)


def _memkv_kernel(mem_ref, g_ref, w_ref, kv_ref):
    kv_ref[...] = _dot(_rmsnorm(mem_ref[...], g_ref[...]).astype(BF16), w_ref[...]).astype(BF16)


def _memkv(mem2, g, w, rows):
    M, D = mem2.shape
    return pl.pallas_call(
        _memkv_kernel,
        grid=(M // rows,),
        in_specs=[pl.BlockSpec((rows, D), lambda i: (i, 0)), pl.BlockSpec((1, D), lambda i: (0, 0)),
                  pl.BlockSpec(w.shape, lambda i: (0, 0))],
        out_specs=pl.BlockSpec((rows, w.shape[1]), lambda i: (i, 0)),
        out_shape=jax.ShapeDtypeStruct((M, w.shape[1]), BF16),
        compiler_params=_cparams(("arbitrary",)),
        name="memkv",
    )(mem2, g, w)


def _xattn_kernel(h_ref, g_ref, wq_ref, kv_ref, wo_ref, o_ref):
    h = h_ref[...]
    q = _dot(_rmsnorm(h, g_ref[...]).astype(BF16), wq_ref[...]).astype(BF16)
    kv = kv_ref[...]
    xw = XATTN_HEADS * XATTN_HEAD_DIM
    scale = XATTN_HEAD_DIM ** -0.5
    outs = []
    for hd in range(XATTN_HEADS):
        sl = slice(hd * XATTN_HEAD_DIM, (hd + 1) * XATTN_HEAD_DIM)
        s = _dot_nt(q[:, sl], kv[:, sl]) * scale
        p = jnp.exp(s - jnp.max(s, axis=-1, keepdims=True))
        p = p / jnp.sum(p, axis=-1, keepdims=True)
        outs.append(_dot(p.astype(BF16), kv[:, xw + hd * XATTN_HEAD_DIM:xw + (hd + 1) * XATTN_HEAD_DIM]))
    o = jnp.concatenate(outs, axis=1).astype(BF16)
    o_ref[...] = h + _dot(o, wo_ref[...])


def _xattn(h1, g, wq, kv, wo, seq, mem_len, tm=512):
    T, D = h1.shape
    per_b = seq // tm
    return pl.pallas_call(
        _xattn_kernel,
        grid=(T // tm,),
        in_specs=[
            pl.BlockSpec((tm, D), lambda i: (i, 0)),
            pl.BlockSpec((1, D), lambda i: (0, 0)),
            pl.BlockSpec(wq.shape, lambda i: (0, 0)),
            pl.BlockSpec((mem_len, kv.shape[1]), lambda i: (i // per_b, 0)),
            pl.BlockSpec(wo.shape, lambda i: (0, 0)),
        ],
        out_specs=pl.BlockSpec((tm, D), lambda i: (i, 0)),
        out_shape=jax.ShapeDtypeStruct((T, D), F32),
        compiler_params=_cparams(("arbitrary",)),
        name="xattn",
    )(h1, g, wq, kv, wo)


def _router_kernel(h_ref, g_ref, wr_ref, br_ref, xn_ref, e_ref, r_ref, gate_ref, cnt_ref, carry_ref):
    i = pl.program_id(0)
    E = N_EXPERTS
    tm = h_ref.shape[0]

    @pl.when(i == 0)
    def _():
        carry_ref[...] = jnp.zeros_like(carry_ref)

    n = _rmsnorm(h_ref[...], g_ref[...])
    _to_row_tiles(xn_ref, n)
    na, nb_, _ = _split3(n)
    wa, wb, _ = _split3(wr_ref[...])
    logits = (_dot_nt(wa, na) + (_dot_nt(wa, nb_) + _dot_nt(wb, na))) + br_ref[...]
    eidx = lax.broadcasted_iota(I32, (E, tm), 0)
    g = logits
    picks, vals = [], []
    for _ in range(EXPERT_TOPK):
        mx = jnp.max(g, axis=0, keepdims=True)
        first = jnp.min(jnp.where(g == mx, eidx, E), axis=0, keepdims=True)
        pick = eidx == first
        picks.append(pick)
        vals.append(mx)
        g = jnp.where(pick, -3e38, g)
    ex = [jnp.exp(v - vals[0]) for v in vals]
    tot = ex[0] + ex[1] + ex[2] + ex[3]
    sel = jnp.zeros((E, tm), F32)
    for pk in picks:
        sel = jnp.where(pk, 1.0, sel)
    before = (lax.broadcasted_iota(I32, (tm, tm), 0) < lax.broadcasted_iota(I32, (tm, tm), 1))
    prefix = _dot(sel.astype(BF16), jnp.where(before, 1.0, 0.0).astype(BF16))
    pos = prefix + carry_ref[:, 0:1]
    zi = jnp.zeros((8 - EXPERT_TOPK, tm), I32)
    e_rows = [jnp.sum(jnp.where(pk, eidx, 0), axis=0, keepdims=True) for pk in picks]
    r_rows = [jnp.sum(jnp.where(pk, pos, 0.0), axis=0, keepdims=True).astype(I32) for pk in picks]
    e_ref[...] = jnp.concatenate(e_rows + [zi], axis=0)
    r_ref[...] = jnp.concatenate(r_rows + [zi], axis=0)
    gate_rows = jnp.concatenate([x / tot for x in ex] + [jnp.zeros((LANES - EXPERT_TOPK, tm), F32)], axis=0)
    gate_ref[...] = gate_rows.T
    carry_ref[...] = carry_ref[...] + jnp.sum(sel, axis=1, keepdims=True)
    cnt_ref[...] = carry_ref[...]


def _router(h2, g, wr_t, br, tm=256):
    T, D = h2.shape
    E = N_EXPERTS
    return pl.pallas_call(
        _router_kernel,
        grid=(T // tm,),
        in_specs=[pl.BlockSpec((tm, D), lambda i: (i, 0)), pl.BlockSpec((1, D), lambda i: (0, 0)),
                  pl.BlockSpec((E, D), lambda i: (0, 0)), pl.BlockSpec((E, 1), lambda i: (0, 0))],
        out_specs=[pl.BlockSpec((tm * ROW_TILE, LANES), lambda i: (i, 0)),
                   pl.BlockSpec((8, tm), lambda i: (0, i)), pl.BlockSpec((8, tm), lambda i: (0, i)),
                   pl.BlockSpec((tm, LANES), lambda i: (i, 0)),
                   pl.BlockSpec((E, LANES), lambda i: (0, 0))],
        out_shape=[jax.ShapeDtypeStruct((T * ROW_TILE, LANES), F32),
                   jax.ShapeDtypeStruct((8, T), I32), jax.ShapeDtypeStruct((8, T), I32),
                   jax.ShapeDtypeStruct((T, LANES), F32),
                   jax.ShapeDtypeStruct((E, LANES), F32)],
        scratch_shapes=[pltpu.VMEM((E, LANES), F32)],
        compiler_params=_cparams(("arbitrary",)),
        name="router",
    )(h2, g, wr_t, br)


def _positions_kernel(ps_ref, e_ref, r_ref, pos_ref):
    e = e_ref[...]
    pos = r_ref[...]
    for x in range(N_EXPERTS):
        pos = pos + jnp.where(e == x, ps_ref[x], 0)
    for c in range(pos_ref.shape[0]):
        pos_ref[c] = pos[:, c * LANES:(c + 1) * LANES]


def _positions(pstart, e8, r8, tm=2048):
    T = e8.shape[1]
    grid_spec = pltpu.PrefetchScalarGridSpec(
        num_scalar_prefetch=1,
        grid=(T // tm,),
        in_specs=[pl.BlockSpec((8, tm), lambda i, ps: (0, i)), pl.BlockSpec((8, tm), lambda i, ps: (0, i))],
        out_specs=pl.BlockSpec((tm // LANES, 8, LANES), lambda i, ps: (i, 0, 0)),
    )
    return pl.pallas_call(
        _positions_kernel, grid_spec=grid_spec, out_shape=jax.ShapeDtypeStruct((T // LANES, 8, LANES), I32),
        compiler_params=_cparams(("arbitrary",)), name="positions",
    )(pstart, e8, r8)


def _row_dmas(tm, make_copy, wait):
    for c in range(tm // LANES):
        for k0 in range(0, EXPERT_TOPK, 2):
            def body(lane, carry, c=c, k0=k0):
                for k in (k0, k0 + 1):
                    cp = make_copy(c, k, lane)
                    cp.wait() if wait else cp.start(priority=k % 2)
                return carry
            lax.fori_loop(0, LANES, body, 0, unroll=4)


def _dispatch_kernel(zr_ref, pos_ref, xn_ref, xs_ref, zbuf, sem, zsem):
    tm = xn_ref.shape[0] // ROW_TILE

    @pl.when(pl.program_id(0) == 0)
    def _():
        zbuf[...] = jnp.zeros_like(zbuf)

        def zero_copy(e):
            first = pl.multiple_of(zr_ref[e] * ROW_TILE, MOE_BLOCK * ROW_TILE)
            return pltpu.make_async_copy(zbuf, xs_ref.at[pl.ds(first, MOE_BLOCK * ROW_TILE)], zsem)

        for e in range(2 * N_EXPERTS):
            @pl.when(zr_ref[e] >= 0)
            def _():
                zero_copy(e).start()
        for e in range(2 * N_EXPERTS):
            @pl.when(zr_ref[e] >= 0)
            def _():
                zero_copy(e).wait()

    row_copy = lambda c, k, lane: pltpu.make_async_copy(
        _tile(xn_ref, c * LANES + lane), _tile(xs_ref, _pos(pos_ref, c, k, lane)), sem)
    _row_dmas(tm, row_copy, wait=False)
    _row_dmas(tm, row_copy, wait=True)


def _dispatch(zrow, pos3, xn, rows_total, tm=1024):
    T = xn.shape[0] // ROW_TILE
    grid_spec = pltpu.PrefetchScalarGridSpec(
        num_scalar_prefetch=1,
        grid=(T // tm,),
        in_specs=[pl.BlockSpec((tm * 8,), lambda i, zr: (i,), memory_space=pltpu.SMEM),
                  pl.BlockSpec((tm * ROW_TILE, LANES), lambda i, zr: (i, 0))],
        out_specs=pl.BlockSpec(memory_space=pl.ANY),
        scratch_shapes=[pltpu.VMEM((MOE_BLOCK * ROW_TILE, LANES), F32), pltpu.SemaphoreType.DMA(()),
                        pltpu.SemaphoreType.DMA(())],
    )
    return pl.pallas_call(
        _dispatch_kernel,
        grid_spec=grid_spec,
        out_shape=jax.ShapeDtypeStruct((rows_total * ROW_TILE, LANES), F32),
        compiler_params=_cparams(("arbitrary",)),
        name="dispatch",
    )(zrow, pos3, xn)


def _experts_kernel(be_ref, nu_ref, xs_ref, wgu_ref, bgu_ref, wdn_ref, bdn_ref, ys_ref, wgu_bf, wdn_bf):
    i = pl.program_id(0)
    dff = wdn_ref.shape[1]

    @pl.when(i < nu_ref[0])
    def _():
        prev = be_ref[jnp.maximum(i - 1, 0)]

        @pl.when((i == 0) | (be_ref[i] != prev))
        def _():
            wgu_bf[...] = wgu_ref[0].astype(BF16)
            wdn_bf[...] = wdn_ref[0].astype(BF16)

        gu = _dot(_from_row_tiles(xs_ref, MOE_BLOCK).astype(BF16), wgu_bf[...]) + bgu_ref[0]
        gate = jnp.minimum(gu[:, :dff], SWIGLU_LIMIT)
        up = jnp.clip(gu[:, dff:], -SWIGLU_LIMIT, SWIGLU_LIMIT)
        hid = (up + 1.0) * gate * jax.nn.sigmoid(SWIGLU_ALPHA * gate)
        _to_row_tiles(ys_ref, _dot(hid.astype(BF16), wdn_bf[...]) + bdn_ref[0])

    @pl.when(i >= nu_ref[0])
    def _():
        ys_ref[...] = jnp.zeros_like(ys_ref)


def _experts(blk_expert, nused, xs, wgu, bgu, wdn, bdn):
    rows = xs.shape[0] // ROW_TILE
    E, D, F2 = wgu.shape
    dff = wdn.shape[1]
    nblk = rows // MOE_BLOCK
    grid_spec = pltpu.PrefetchScalarGridSpec(
        num_scalar_prefetch=2,
        grid=(nblk,),
        in_specs=[pl.BlockSpec((MOE_BLOCK * ROW_TILE, LANES), lambda i, be, nu: (jnp.minimum(i, nu[0] - 1), 0)),
                  pl.BlockSpec((1, D, F2), lambda i, be, nu: (be[i], 0, 0)),
                  pl.BlockSpec((1, 1, F2), lambda i, be, nu: (be[i], 0, 0)),
                  pl.BlockSpec((1, dff, D), lambda i, be, nu: (be[i], 0, 0)),
                  pl.BlockSpec((1, 1, D), lambda i, be, nu: (be[i], 0, 0))],
        out_specs=pl.BlockSpec((MOE_BLOCK * ROW_TILE, LANES), lambda i, be, nu: (i, 0)),
        scratch_shapes=[pltpu.VMEM((D, F2), BF16), pltpu.VMEM((dff, D), BF16)],
    )
    return pl.pallas_call(
        _experts_kernel,
        grid_spec=grid_spec,
        out_shape=jax.ShapeDtypeStruct((rows * ROW_TILE, LANES), F32),
        compiler_params=_cparams(("arbitrary",)),
        name="experts",
    )(blk_expert, nused, xs, wgu, bgu, wdn, bdn)


def _combine_kernel(pos_ref, nxt_ref, h_ref, gate_ref, gf_ref, ys_ref, o_ref, buf, sem):
    i = pl.program_id(0)
    tm = h_ref.shape[0]
    slot = i % 2

    def gather(p_ref, slot):
        first = slot * (EXPERT_TOPK * tm)
        return lambda c, k, lane: pltpu.make_async_copy(
            _tile(ys_ref, _pos(p_ref, c, k, lane)), _tile(buf, first + k * tm + c * LANES + lane), sem.at[slot])

    @pl.when(i == 0)
    def _():
        _row_dmas(tm, gather(pos_ref, slot), wait=False)

    @pl.when(i + 1 < pl.num_programs(0))
    def _():
        _row_dmas(tm, gather(nxt_ref, 1 - slot), wait=False)

    _row_dmas(tm, gather(pos_ref, slot), wait=True)
    gates = gate_ref[...]
    acc = h_ref[...]
    for k in range(EXPERT_TOPK):
        acc = acc + gates[:, k:k + 1] * _from_row_tiles(buf, tm, first=slot * (EXPERT_TOPK * tm) + k * tm)
    o_ref[...] = _rmsnorm(acc, gf_ref[...])


def _combine(pos3, h2, gates, gf, ys, tm=256):
    T, D = h2.shape
    return pl.pallas_call(
        _combine_kernel,
        grid=(T // tm,),
        in_specs=[pl.BlockSpec((tm * 8,), lambda i: (i,), memory_space=pltpu.SMEM),
                  pl.BlockSpec((tm * 8,), lambda i: (jnp.minimum(i + 1, T // tm - 1),), memory_space=pltpu.SMEM),
                  pl.BlockSpec((tm, D), lambda i: (i, 0)),
                  pl.BlockSpec((tm, LANES), lambda i: (i, 0)),
                  pl.BlockSpec((1, D), lambda i: (0, 0)),
                  pl.BlockSpec(memory_space=pl.ANY)],
        out_specs=pl.BlockSpec((tm, D), lambda i: (i, 0)),
        out_shape=jax.ShapeDtypeStruct((T, D), F32),
        scratch_shapes=[pltpu.VMEM((2 * EXPERT_TOPK * tm * ROW_TILE, LANES), F32), pltpu.SemaphoreType.DMA((2,))],
        compiler_params=_cparams(("arbitrary",)),
        name="combine",
    )(pos3, pos3, h2, gates, gf, ys)


def _pad_lanes(a, width=LANES, value=0.0):
    pad = [(0, 0)] * (a.ndim - 1) + [(0, width - a.shape[-1])]
    return jnp.pad(a, pad, constant_values=value)


def _layer(h2d, mem2d, batch, seq, mem_len, g_mix, w_in, a_re, a_im, log_dt, b_re, b_im, c_re, c_im, d_skip,
           w_glu, w_proj_a, w_out, g_xattn, g_mem, w_xq, w_xkv, w_xo, g_moe, w_router, b_router,
           w_gu, b_gu, w_dn, b_dn, g_final):
    T, D = h2d.shape
    aw = ATT_HEADS * ATT_HEAD_DIM
    G = a_re.shape[0]
    sw = G * SSM_GROUP

    wq, wk, wv, wu, wga, wgb = jnp.split(w_in, [aw, 2 * aw, 3 * aw, 3 * aw + sw, 3 * aw + sw + D], axis=1)
    wg_t = jnp.concatenate([wga, wgb, wu], axis=1).T.astype(BF16)
    wqv_t = jnp.concatenate([wq * (ATT_HEAD_DIM ** -0.5), wv], axis=1).T.astype(BF16)
    wk_pad = _pad_lanes(wk.reshape(D, ATT_HEADS, ATT_HEAD_DIM)).reshape(D, ATT_HEADS * LANES).astype(BF16)

    zt, q4, v4, kpad, km = _in_proj(h2d, g_mix.reshape(1, D), wg_t, wqv_t, wk_pad)
    nb = seq // MOBA_BLOCK
    att4 = _moba(q4, v4, kpad.reshape(T // MOBA_BLOCK, MOBA_BLOCK, ATT_HEADS * LANES),
                 km.reshape(batch, nb, ATT_HEADS * LANES), batch)

    row = lambda a, v=0.0: _pad_lanes(a, value=v).reshape(G, 1, LANES)
    col = lambda a, v=0.0: _pad_lanes(a, value=v).reshape(G, LANES, 1)
    ssm_params = (
        row(a_re, -1.0), row(a_im), log_dt.reshape(G, 1, 1), col(a_re, -1.0), col(a_im),
        _pad_lanes(jnp.swapaxes(b_re, 1, 2)), _pad_lanes(jnp.swapaxes(b_im, 1, 2)),
        _pad_lanes(c_re), _pad_lanes(c_im),
        _pad_lanes(c_re)[..., None], _pad_lanes(c_im)[..., None],
        d_skip.reshape(G, SSM_GROUP, 1, 1),
    )
    u3 = zt[2 * D:].reshape(sw, T // SSM_CHUNK, SSM_CHUNK)
    ys3 = _ssm(u3, 0, ssm_params, batch)
    h1 = _mixer_out(zt, ys3.reshape(sw, T), att4, h2d, w_glu.T.astype(BF16), w_proj_a.T.astype(BF16),
                    w_out.astype(BF16))

    kv = _memkv(mem2d, g_mem.reshape(1, D), w_xkv.astype(BF16), mem_len)
    h2 = _xattn(h1, g_xattn.reshape(1, D), w_xq.astype(BF16), kv, w_xo.astype(BF16), seq, mem_len)

    xn, e8, r8, gates, cnt = _router(h2, g_moe.reshape(1, D), w_router.T, b_router.reshape(N_EXPERTS, 1))
    counts = cnt[:, 0].astype(I32)
    nblk_e = (counts + MOE_BLOCK - 1) // MOE_BLOCK
    bends = jnp.cumsum(nblk_e)
    pstart = ((bends - nblk_e) * MOE_BLOCK).astype(I32)
    nblk = (T * EXPERT_TOPK) // MOE_BLOCK + N_EXPERTS
    blk_expert = jnp.minimum(jnp.sum(bends[None, :] <= jnp.arange(nblk, dtype=I32)[:, None], axis=1),
                             N_EXPERTS - 1).astype(I32)
    nused = bends[-1:].astype(I32)
    blk_expert = jnp.where(jnp.arange(nblk) < nused[0], blk_expert, blk_expert[jnp.maximum(nused[0] - 1, 0)])

    tail = nblk - 1 - jnp.arange(N_EXPERTS, dtype=I32)
    zrow = jnp.concatenate([jnp.where(nblk_e > 0, (bends - 1) * MOE_BLOCK, -1),
                            jnp.where(tail >= nused[0], tail * MOE_BLOCK, -1)]).astype(I32)
    pos3 = _positions(pstart, e8, r8).reshape(-1)
    xs = _dispatch(zrow, pos3, xn, nblk * MOE_BLOCK)
    ys = _experts(blk_expert, nused, xs, w_gu, b_gu.reshape(N_EXPERTS, 1, -1), w_dn, b_dn.reshape(N_EXPERTS, 1, -1))
    return _combine(pos3, h2, gates, g_final.reshape(1, D), ys)


def kernel(x, mem, g_mix, w_in, a_re, a_im, log_dt, b_re, b_im, c_re, c_im, d_skip, w_glu, w_proj_a, w_out,
           g_xattn, g_mem, w_xq, w_xkv, w_xo, g_moe, w_router, b_router, w_gu, b_gu, w_dn, b_dn, g_final):
    B, S, D = x.shape
    M = mem.shape[1]
    assert g_mix.shape[0] == 1, "single layer"
    out = _layer(x.reshape(B * S, D), mem.reshape(B * M, D), B, S, M,
                 g_mix[0], w_in[0], a_re[0], a_im[0], log_dt[0], b_re[0], b_im[0], c_re[0], c_im[0], d_skip[0],
                 w_glu[0], w_proj_a[0], w_out[0], g_xattn[0], g_mem[0], w_xq[0], w_xkv[0], w_xo[0],
                 g_moe[0], w_router[0], b_router[0], w_gu[0], b_gu[0], w_dn[0], b_dn[0], g_final)
    return out.reshape(B, S, D)
```

```python
import functools
import math

import jax
import jax.numpy as jnp
from jax import lax
from jax.experimental import pallas as pl
from jax.experimental.pallas import tpu as pltpu

F32 = jnp.float32
BF16 = jnp.bfloat16
I32 = jnp.int32

RMS_EPS = 1e-6
NEG_INF = -1e30
ATT_HEADS = 8
ATT_HEAD_DIM = 64
MOBA_BLOCK = 256
MOBA_TOPK = 3
MOBA_HEADS_PER_STEP = 4
SSM_GROUP = 16
SSM_STATE = 64
SSM_CHUNK = 128
XATTN_HEADS = 4
XATTN_HEAD_DIM = 128
N_EXPERTS = 32
EXPERT_TOPK = 4
SWIGLU_LIMIT = 7.0
SWIGLU_ALPHA = 1.702
MOE_BLOCK = 512
LANES = 128
VMEM_LIMIT = 56 * 1024 * 1024


def _cparams(sem):
    return pltpu.CompilerParams(dimension_semantics=sem, vmem_limit_bytes=VMEM_LIMIT)


def _rmsnorm(x, g):
    return x * lax.rsqrt(jnp.mean(x * x, axis=-1, keepdims=True) + RMS_EPS) * g


def _dot(a, b):
    return jnp.dot(a, b, preferred_element_type=F32)


def _dot_nt(a, b):
    return lax.dot_general(a, b, (((1,), (1,)), ((), ())), preferred_element_type=F32)


def _dot_tn(a, b):
    return lax.dot_general(a, b, (((0,), (0,)), ((), ())), preferred_element_type=F32)


def _split3(x):
    a = x.astype(BF16)
    r = x - a.astype(F32)
    b = r.astype(BF16)
    c = (r - b.astype(F32)).astype(BF16)
    return a, b, c


ROW_TILE = 8


def _to_row_tiles(ref, x, first=0):
    for s in range(ROW_TILE):
        ref[pl.ds(first * ROW_TILE + s, x.shape[0], stride=ROW_TILE), :] = x[:, s * LANES:(s + 1) * LANES]


def _from_row_tiles(ref, rows, first=0):
    return jnp.concatenate(
        [ref[pl.ds(first * ROW_TILE + s, rows, stride=ROW_TILE), :] for s in range(ROW_TILE)], axis=1)


def _pos(pos_ref, c, k, lane):
    return pos_ref[(c * 8 + k) * LANES + lane]


def _tile(ref, r):
    return ref.at[pl.ds(pl.multiple_of(r * ROW_TILE, ROW_TILE), ROW_TILE)]


def _inproj_kernel(x_ref, g_ref, wg_ref, wqv_ref, wk_ref, zt_ref, q_ref, v_ref, k_ref, km_ref):
    n = _rmsnorm(x_ref[...], g_ref[...]).astype(BF16)
    tm = n.shape[0]
    nblk = tm // MOBA_BLOCK
    rows = 512
    for c in range(wg_ref.shape[0] // rows):
        zt_ref[c * rows:(c + 1) * rows, :] = _dot_nt(wg_ref[c * rows:(c + 1) * rows, :], n).astype(BF16)
    for c, dst in ((0, q_ref), (1, v_ref)):
        r = _dot_nt(wqv_ref[c * rows:(c + 1) * rows, :], n).astype(BF16)
        for h in range(ATT_HEADS):
            for j in range(nblk):
                dst[h, j] = r[h * ATT_HEAD_DIM:(h + 1) * ATT_HEAD_DIM, j * MOBA_BLOCK:(j + 1) * MOBA_BLOCK]
    k = _dot(n, wk_ref[...])
    lane = lax.broadcasted_iota(I32, (1, k.shape[1]), 1) & (LANES - 1)
    k = k + jnp.where(lane == ATT_HEAD_DIM, 1.0, 0.0)
    k_ref[...] = k.astype(BF16)
    km_ref[0] = jnp.mean(k.reshape(nblk, MOBA_BLOCK, k.shape[1]), axis=1)


def _in_proj(x2, g, wg_t, wqv_t, wk_pad, tm=1024):
    T, D = x2.shape
    nblk = tm // MOBA_BLOCK
    ng = wg_t.shape[0]
    kw = wk_pad.shape[1]
    hd = (ATT_HEADS, T // MOBA_BLOCK, ATT_HEAD_DIM, MOBA_BLOCK)
    return pl.pallas_call(
        _inproj_kernel,
        grid=(T // tm,),
        in_specs=[
            pl.BlockSpec((tm, D), lambda i: (i, 0)),
            pl.BlockSpec((1, D), lambda i: (0, 0)),
            pl.BlockSpec((ng, D), lambda i: (0, 0)),
            pl.BlockSpec(wqv_t.shape, lambda i: (0, 0)),
            pl.BlockSpec((D, kw), lambda i: (0, 0)),
        ],
        out_specs=[
            pl.BlockSpec((ng, tm), lambda i: (0, i)),
            pl.BlockSpec((ATT_HEADS, nblk, ATT_HEAD_DIM, MOBA_BLOCK), lambda i: (0, i, 0, 0)),
            pl.BlockSpec((ATT_HEADS, nblk, ATT_HEAD_DIM, MOBA_BLOCK), lambda i: (0, i, 0, 0)),
            pl.BlockSpec((tm, kw), lambda i: (i, 0)),
            pl.BlockSpec((1, nblk, kw), lambda i: (i, 0, 0)),
        ],
        out_shape=[
            jax.ShapeDtypeStruct((ng, T), BF16),
            jax.ShapeDtypeStruct(hd, BF16),
            jax.ShapeDtypeStruct(hd, BF16),
            jax.ShapeDtypeStruct((T, kw), BF16),
            jax.ShapeDtypeStruct((T // tm, nblk, kw), F32),
        ],
        compiler_params=_cparams(("arbitrary",)),
        name="in_proj",
    )(x2, g, wg_t, wqv_t, wk_pad)


def _moba_kernel(q_ref, v_ref, k_ref, km_ref, o_ref, sel_ref, acc_ref, sa_ref, sb_ref, pb_ref, *, nb, hb):
    blk = MOBA_BLOCK
    heads = range(hb)
    lanes = lambda h: slice(h * LANES, (h + 1) * LANES)
    kms = [_split3(km_ref[0][:, lanes(h)]) for h in heads]
    zpad = jnp.zeros((LANES - ATT_HEAD_DIM, blk), BF16)
    bidx = lax.broadcasted_iota(I32, (nb, blk), 0)
    causal = lax.broadcasted_iota(I32, (blk, blk), 0) <= lax.broadcasted_iota(I32, (blk, blk), 1)
    topk = min(MOBA_TOPK, nb)

    row16 = lax.broadcasted_iota(I32, (16, blk), 0) == 0
    ones16 = jnp.where(row16, 1.0, 0.0).astype(BF16)
    zpad48 = jnp.zeros((LANES - ATT_HEAD_DIM - 16, blk), BF16)

    def qblock(i, carry):
        elig = bidx < i
        q64 = [q_ref[h, i] for h in heads]
        for h in heads:
            q = jnp.concatenate([q64[h], zpad], axis=0)
            gate = (_dot(kms[h][0], q) + _dot(kms[h][1], q)) + _dot(kms[h][2], q)
            g = jnp.where(elig, gate, NEG_INF)
            sel = jnp.zeros((nb, blk), F32)
            for _ in range(topk):
                mx = jnp.max(g, axis=0, keepdims=True)
                first = jnp.min(jnp.where(g == mx, bidx, nb), axis=0, keepdims=True)
                pick = bidx == first
                sel = jnp.where(pick, 1.0, sel)
                g = jnp.where(pick, -3e38, g)
            sel_ref[h] = jnp.where(elig, sel, 0.0)
            acc_ref[h] = jnp.zeros((ATT_HEAD_DIM + 16, blk), F32)
            pb_ref[h] = jnp.zeros((blk, blk), BF16)

        def scores(h, t):
            bias = (sel_ref[h, pl.ds(t, 1), :] - 1.0) * 1e30
            q = jnp.concatenate([q64[h], jnp.where(row16, bias, 0.0).astype(BF16), zpad48], axis=0)
            return _dot(k_ref[t][:, lanes(h)], q)

        def values(h, t, p):
            return _dot(jnp.concatenate([v_ref[h, t], ones16], axis=0), p)

        def softmax(s, m):
            m_new = jnp.maximum(m, jnp.max(s, axis=0, keepdims=True))
            return m_new, jnp.exp(m - m_new), jnp.exp(s - m_new).astype(BF16)

        for h in heads:
            sa_ref[h] = scores(h, 0)

        def kvpair(u, c):
            ta, tb, tn = 2 * u, 2 * u + 1, jnp.minimum(2 * u + 2, nb - 1)
            pv = [values(h, jnp.maximum(ta - 1, 0), pb_ref[h]) for h in heads]
            sb = [scores(h, tb) for h in heads]
            for h in heads:
                sb_ref[h] = sb[h]
                acc_ref[h] = acc_ref[h] * c[2 * h + 1] + pv[h]
            sm = [softmax(sa_ref[h], c[2 * h]) for h in heads]
            pv = [values(h, ta, sm[h][2]) for h in heads]
            sa = [scores(h, tn) for h in heads]
            out = []
            for h in heads:
                sa_ref[h] = sa[h]
                acc_ref[h] = acc_ref[h] * sm[h][1] + pv[h]
                m_new, alpha, p = softmax(sb_ref[h], sm[h][0])
                pb_ref[h] = p
                out += [m_new, alpha]
            return tuple(out)

        m0 = jnp.full((1, blk), -1e29, F32)
        npair = (i + 1) // 2
        c = lax.fori_loop(0, npair, kvpair, (m0, jnp.ones((1, blk), F32)) * hb)
        pv = [values(h, jnp.maximum(2 * npair - 1, 0), pb_ref[h]) for h in heads]
        kd = k_ref[i]
        own = [_dot(kd[:, lanes(h)], jnp.concatenate([q64[h], zpad], axis=0)) for h in heads]
        sm = [softmax(jnp.where(causal, own[h], NEG_INF), c[2 * h]) for h in heads]
        for h in heads:
            res = (acc_ref[h] * c[2 * h + 1] + pv[h]) * sm[h][1] + values(h, i, sm[h][2])
            o_ref[h, i] = (res[:ATT_HEAD_DIM] / res[ATT_HEAD_DIM:ATT_HEAD_DIM + 1]).astype(BF16)
        return carry

    lax.fori_loop(0, nb, qblock, 0)


def _moba(q4, v4, k3, km3, batch, hb=MOBA_HEADS_PER_STEP):
    H, nblk_total, dh, blk = q4.shape
    nb = nblk_total // batch
    return pl.pallas_call(
        functools.partial(_moba_kernel, nb=nb, hb=hb),
        grid=(batch, H // hb),
        in_specs=[
            pl.BlockSpec((hb, nb, dh, blk), lambda b, h: (h, b, 0, 0)),
            pl.BlockSpec((hb, nb, dh, blk), lambda b, h: (h, b, 0, 0)),
            pl.BlockSpec((nb, blk, hb * LANES), lambda b, h: (b, 0, h)),
            pl.BlockSpec((1, nb, hb * LANES), lambda b, h: (b, 0, h)),
        ],
        out_specs=pl.BlockSpec((hb, nb, dh, blk), lambda b, h: (h, b, 0, 0)),
        out_shape=jax.ShapeDtypeStruct(q4.shape, BF16),
        scratch_shapes=[pltpu.VMEM((hb, nb, blk), F32),
                        pltpu.VMEM((hb, dh + 16, blk), F32),
                        pltpu.VMEM((hb, blk, blk), F32),
                        pltpu.VMEM((hb, blk, blk), F32),
                        pltpu.VMEM((hb, blk, blk), BF16)],
        compiler_params=_cparams(("arbitrary", "arbitrary")),
        name="moba",
    )(q4, v4, k3, km3)


def _ssm_kernel(u_ref, ar_ref, ai_ref, ldt_ref, arc_ref, aic_ref, btr_ref, bti_ref, cr_ref, ci_ref,
                ctr_ref, cti_ref, d_ref, y_ref,
                u2_ref, m_ref, pr_ref, pi_ref, qr_ref, qi_ref, kv_ref, sr_ref, si_ref, xr_ref, xi_ref,
                *, nbatch):
    L = SSM_CHUNK
    C = SSM_GROUP
    N = u_ref.shape[1]
    nchunk = N // nbatch
    hi = lax.Precision.HIGHEST

    dt = jnp.exp(ldt_ref[0])
    are = jnp.minimum(ar_ref[0], -1e-4)
    aim = ai_ref[0]
    ea, th = are * dt, aim * dt
    mag = jnp.exp(ea)
    lbr, lbi = mag * jnp.cos(th), mag * jnp.sin(th)
    den = are * are + aim * aim
    cfr = ((lbr - 1.0) * are + lbi * aim) / den
    cfi = (lbi * are - (lbr - 1.0) * aim) / den
    btr, bti = btr_ref[0], bti_ref[0]
    bbr = cfr * btr - cfi * bti
    bbi = cfr * bti + cfi * btr
    crr, cri = cr_ref[0], ci_ref[0]

    eac = jnp.minimum(arc_ref[0], -1e-4) * dt
    thc = aic_ref[0] * dt
    tau = lax.broadcasted_iota(I32, (1, L), 1).astype(F32)
    m0 = jnp.exp(eac * tau)
    pw0r, pw0i = m0 * jnp.cos(thc * tau), m0 * jnp.sin(thc * tau)
    m1 = jnp.exp(eac * (tau + 1.0))
    pw1r, pw1i = m1 * jnp.cos(thc * (tau + 1.0)), m1 * jnp.sin(thc * (tau + 1.0))
    back = (L - 1.0) - lax.broadcasted_iota(I32, (L, 1), 0).astype(F32)
    mb = jnp.exp(ea * back)
    pbr, pbi = mb * jnp.cos(th * back), mb * jnp.sin(th * back)
    ml = jnp.exp(ea * float(L))
    alr, ali = ml * jnp.cos(th * float(L)), ml * jnp.sin(th * float(L))

    upper = lax.broadcasted_iota(I32, (L, L), 1) >= lax.broadcasted_iota(I32, (L, L), 0)
    for ci in range(C):
        cbr = bbr[ci:ci + 1, :] * crr - bbi[ci:ci + 1, :] * cri
        cbi = bbr[ci:ci + 1, :] * cri + bbi[ci:ci + 1, :] * crr
        kv_ref[ci * C:(ci + 1) * C, :] = (jnp.dot(cbr, pw0r, precision=hi, preferred_element_type=F32)
                                          - jnp.dot(cbi, pw0i, precision=hi, preferred_element_type=F32))
        pr_ref[ci * L:(ci + 1) * L, :] = (pbr * bbr[ci:ci + 1, :] - pbi * bbi[ci:ci + 1, :]).astype(BF16)
        pi_ref[ci * L:(ci + 1) * L, :] = (pbr * bbi[ci:ci + 1, :] + pbi * bbr[ci:ci + 1, :]).astype(BF16)
        u2_ref[:, ci * L:(ci + 1) * L] = u_ref[ci]
    for co in range(C):
        cc_r, cc_i = ctr_ref[0, co], cti_ref[0, co]
        qr_ref[:, co * L:(co + 1) * L] = (cc_r * pw1r - cc_i * pw1i).astype(BF16)
        qi_ref[:, co * L:(co + 1) * L] = (-(cc_r * pw1i + cc_i * pw1r)).astype(BF16)

    wide = 4 * L
    nwide = C * L // wide

    def toeplitz(n0):
        for ci in range(C):
            for co in range(n0 * (wide // L), (n0 + 1) * (wide // L)):
                taps = kv_ref[ci * C + co:ci * C + co + 1, :]
                t = pltpu.roll(jnp.broadcast_to(taps, (L, L)), 0, 1, stride=1, stride_axis=0)
                m_ref[ci * L:(ci + 1) * L, co * L:(co + 1) * L] = jnp.where(upper, t, 0.0).astype(BF16)

    u2 = u2_ref[...]
    sr_ref[...] = _dot(u2, pr_ref[...]).reshape(nbatch, nchunk, LANES)
    si_ref[...] = _dot(u2, pi_ref[...]).reshape(nbatch, nchunk, LANES)
    toeplitz(0)

    def chunk_scan(c, carry):
        xr, xi = carry
        xr_ref[:, pl.ds(c, 1), :] = xr
        xi_ref[:, pl.ds(c, 1), :] = xi
        nr = alr * xr - ali * xi + sr_ref[:, pl.ds(c, 1), :]
        ni = alr * xi + ali * xr + si_ref[:, pl.ds(c, 1), :]
        return nr, ni

    zero = jnp.zeros((nbatch, 1, LANES), F32)
    lax.fori_loop(0, nchunk, chunk_scan, (zero, zero))
    xpr = xr_ref[...].reshape(N, LANES).astype(BF16)
    xpi = xi_ref[...].reshape(N, LANES).astype(BF16)

    for n0 in range(nwide):
        y = (_dot(u2, m_ref[:, n0 * wide:(n0 + 1) * wide])
             + _dot(xpr, qr_ref[:, n0 * wide:(n0 + 1) * wide])
             + _dot(xpi, qi_ref[:, n0 * wide:(n0 + 1) * wide]))
        if n0 + 1 < nwide:
            toeplitz(n0 + 1)
        for cc in range(wide // L):
            co = n0 * (wide // L) + cc
            y_ref[co] = (y[:, cc * L:(cc + 1) * L] + d_ref[0, co] * u_ref[co].astype(F32)).astype(BF16)


def _ssm(zt3, row0, params, nbatch):
    (ar, ai, ldt, arc, aic, btr, bti, cr, ci, ctr, cti, d) = params
    G = ar.shape[0]
    C, L = SSM_GROUP, SSM_CHUNK
    N = zt3.shape[1]
    g3 = lambda g: (g, 0, 0)
    g4 = lambda g: (g, 0, 0, 0)
    return pl.pallas_call(
        functools.partial(_ssm_kernel, nbatch=nbatch),
        grid=(G,),
        in_specs=[
            pl.BlockSpec((C, N, L), lambda g: (row0 // C + g, 0, 0)),
            pl.BlockSpec((1, 1, LANES), g3), pl.BlockSpec((1, 1, LANES), g3), pl.BlockSpec((1, 1, 1), g3),
            pl.BlockSpec((1, LANES, 1), g3), pl.BlockSpec((1, LANES, 1), g3),
            pl.BlockSpec((1, C, LANES), g3), pl.BlockSpec((1, C, LANES), g3),
            pl.BlockSpec((1, C, LANES), g3), pl.BlockSpec((1, C, LANES), g3),
            pl.BlockSpec((1, C, LANES, 1), g4), pl.BlockSpec((1, C, LANES, 1), g4),
            pl.BlockSpec((1, C, 1, 1), g4),
        ],
        out_specs=pl.BlockSpec((C, N, L), g3),
        out_shape=jax.ShapeDtypeStruct((G * C, N, L), BF16),
        scratch_shapes=[
            pltpu.VMEM((N, C * L), BF16),
            pltpu.VMEM((C * L, C * L), BF16),
            pltpu.VMEM((C * L, LANES), BF16), pltpu.VMEM((C * L, LANES), BF16),
            pltpu.VMEM((LANES, C * L), BF16), pltpu.VMEM((LANES, C * L), BF16),
            pltpu.VMEM((C * C, L), F32),
            pltpu.VMEM((nbatch, N // nbatch, LANES), F32), pltpu.VMEM((nbatch, N // nbatch, LANES), F32),
            pltpu.VMEM((nbatch, N // nbatch, LANES), F32), pltpu.VMEM((nbatch, N // nbatch, LANES), F32),
        ],
        compiler_params=_cparams(("arbitrary",)),
        name="ssm",
    )(zt3, ar, ai, ldt, arc, aic, btr, bti, cr, ci, ctr, cti, d)


def _mixer_out_kernel(ga_ref, gb_ref, ys_ref, at_ref, x_ref, wglu_ref, wpa_ref, wout_ref, h_ref):
    dm = ga_ref.shape[0]
    nblk = at_ref.shape[1]
    att = jnp.concatenate(
        [jnp.concatenate([at_ref[h, j] for j in range(nblk)], axis=1) for h in range(ATT_HEADS)], axis=0)
    ya = _dot(wpa_ref[...], att)
    gl = jax.nn.gelu(ys_ref[...].astype(F32)).astype(BF16)
    zb = _dot(wglu_ref[...], gl)
    yb = zb[:dm] * jax.nn.sigmoid(zb[dm:])
    merged = jax.nn.sigmoid(ga_ref[...].astype(F32)) * ya + jax.nn.sigmoid(gb_ref[...].astype(F32)) * yb
    h_ref[...] = x_ref[...] + _dot_tn(merged.astype(BF16), wout_ref[...])


def _mixer_out(zt, ys_t, att4, x2, wglu_t, wpa_t, wout, tm=512):
    T, D = x2.shape
    nblk = tm // MOBA_BLOCK
    sw = ys_t.shape[0]
    full = lambda a: pl.BlockSpec(a.shape, lambda i: (0,) * a.ndim)
    return pl.pallas_call(
        _mixer_out_kernel,
        grid=(T // tm,),
        in_specs=[
            pl.BlockSpec((D, tm), lambda i: (0, i)),
            pl.BlockSpec((D, tm), lambda i: (1, i)),
            pl.BlockSpec((sw, tm), lambda i: (0, i)),
            pl.BlockSpec((ATT_HEADS, nblk, ATT_HEAD_DIM, MOBA_BLOCK), lambda i: (0, i, 0, 0)),
            pl.BlockSpec((tm, D), lambda i: (i, 0)),
            full(wglu_t), full(wpa_t), full(wout),
        ],
        out_specs=pl.BlockSpec((tm, D), lambda i: (i, 0)),
        out_shape=jax.ShapeDtypeStruct((T, D), F32),
        compiler_params=_cparams(("arbitrary",)),
        name="mixer_out",
    )(zt, zt, ys_t, att4, x2, wglu_t, wpa_t, wout)


def _memkv_kernel(mem_ref, g_ref, w_ref, kv_ref):
    kv_ref[...] = _dot(_rmsnorm(mem_ref[...], g_ref[...]).astype(BF16), w_ref[...]).astype(BF16)


def _memkv(mem2, g, w, rows):
    M, D = mem2.shape
    return pl.pallas_call(
        _memkv_kernel,
        grid=(M // rows,),
        in_specs=[pl.BlockSpec((rows, D), lambda i: (i, 0)), pl.BlockSpec((1, D), lambda i: (0, 0)),
                  pl.BlockSpec(w.shape, lambda i: (0, 0))],
        out_specs=pl.BlockSpec((rows, w.shape[1]), lambda i: (i, 0)),
        out_shape=jax.ShapeDtypeStruct((M, w.shape[1]), BF16),
        compiler_params=_cparams(("arbitrary",)),
        name="memkv",
    )(mem2, g, w)


def _xattn_kernel(h_ref, g_ref, wq_ref, kv_ref, wo_ref, o_ref):
    h = h_ref[...]
    q = _dot(_rmsnorm(h, g_ref[...]).astype(BF16), wq_ref[...]).astype(BF16)
    kv = kv_ref[...]
    xw = XATTN_HEADS * XATTN_HEAD_DIM
    scale = XATTN_HEAD_DIM ** -0.5
    outs = []
    for hd in range(XATTN_HEADS):
        sl = slice(hd * XATTN_HEAD_DIM, (hd + 1) * XATTN_HEAD_DIM)
        s = _dot_nt(q[:, sl], kv[:, sl]) * scale
        p = jnp.exp(s - jnp.max(s, axis=-1, keepdims=True))
        p = p / jnp.sum(p, axis=-1, keepdims=True)
        outs.append(_dot(p.astype(BF16), kv[:, xw + hd * XATTN_HEAD_DIM:xw + (hd + 1) * XATTN_HEAD_DIM]))
    o = jnp.concatenate(outs, axis=1).astype(BF16)
    o_ref[...] = h + _dot(o, wo_ref[...])


def _xattn(h1, g, wq, kv, wo, seq, mem_len, tm=512):
    T, D = h1.shape
    per_b = seq // tm
    return pl.pallas_call(
        _xattn_kernel,
        grid=(T // tm,),
        in_specs=[
            pl.BlockSpec((tm, D), lambda i: (i, 0)),
            pl.BlockSpec((1, D), lambda i: (0, 0)),
            pl.BlockSpec(wq.shape, lambda i: (0, 0)),
            pl.BlockSpec((mem_len, kv.shape[1]), lambda i: (i // per_b, 0)),
            pl.BlockSpec(wo.shape, lambda i: (0, 0)),
        ],
        out_specs=pl.BlockSpec((tm, D), lambda i: (i, 0)),
        out_shape=jax.ShapeDtypeStruct((T, D), F32),
        compiler_params=_cparams(("arbitrary",)),
        name="xattn",
    )(h1, g, wq, kv, wo)


def _router_kernel(h_ref, g_ref, wr_ref, br_ref, xn_ref, e_ref, r_ref, gate_ref, cnt_ref, carry_ref):
    i = pl.program_id(0)
    E = N_EXPERTS
    tm = h_ref.shape[0]

    @pl.when(i == 0)
    def _():
        carry_ref[...] = jnp.zeros_like(carry_ref)

    n = _rmsnorm(h_ref[...], g_ref[...])
    _to_row_tiles(xn_ref, n)
    na, nb_, _ = _split3(n)
    wa, wb, _ = _split3(wr_ref[...])
    logits = (_dot_nt(wa, na) + (_dot_nt(wa, nb_) + _dot_nt(wb, na))) + br_ref[...]
    eidx = lax.broadcasted_iota(I32, (E, tm), 0)
    g = logits
    picks, vals = [], []
    for _ in range(EXPERT_TOPK):
        mx = jnp.max(g, axis=0, keepdims=True)
        first = jnp.min(jnp.where(g == mx, eidx, E), axis=0, keepdims=True)
        pick = eidx == first
        picks.append(pick)
        vals.append(mx)
        g = jnp.where(pick, -3e38, g)
    ex = [jnp.exp(v - vals[0]) for v in vals]
    tot = ex[0] + ex[1] + ex[2] + ex[3]
    sel = jnp.zeros((E, tm), F32)
    for pk in picks:
        sel = jnp.where(pk, 1.0, sel)
    before = (lax.broadcasted_iota(I32, (tm, tm), 0) < lax.broadcasted_iota(I32, (tm, tm), 1))
    prefix = _dot(sel.astype(BF16), jnp.where(before, 1.0, 0.0).astype(BF16))
    pos = prefix + carry_ref[:, 0:1]
    zi = jnp.zeros((8 - EXPERT_TOPK, tm), I32)
    e_rows = [jnp.sum(jnp.where(pk, eidx, 0), axis=0, keepdims=True) for pk in picks]
    r_rows = [jnp.sum(jnp.where(pk, pos, 0.0), axis=0, keepdims=True).astype(I32) for pk in picks]
    e_ref[...] = jnp.concatenate(e_rows + [zi], axis=0)
    r_ref[...] = jnp.concatenate(r_rows + [zi], axis=0)
    gate_rows = jnp.concatenate([x / tot for x in ex] + [jnp.zeros((LANES - EXPERT_TOPK, tm), F32)], axis=0)
    gate_ref[...] = gate_rows.T
    carry_ref[...] = carry_ref[...] + jnp.sum(sel, axis=1, keepdims=True)
    cnt_ref[...] = carry_ref[...]


def _router(h2, g, wr_t, br, tm=512):
    T, D = h2.shape
    E = N_EXPERTS
    return pl.pallas_call(
        _router_kernel,
        grid=(T // tm,),
        in_specs=[pl.BlockSpec((tm, D), lambda i: (i, 0)), pl.BlockSpec((1, D), lambda i: (0, 0)),
                  pl.BlockSpec((E, D), lambda i: (0, 0)), pl.BlockSpec((E, 1), lambda i: (0, 0))],
        out_specs=[pl.BlockSpec((tm * ROW_TILE, LANES), lambda i: (i, 0)),
                   pl.BlockSpec((8, tm), lambda i: (0, i)), pl.BlockSpec((8, tm), lambda i: (0, i)),
                   pl.BlockSpec((tm, LANES), lambda i: (i, 0)),
                   pl.BlockSpec((E, LANES), lambda i: (0, 0))],
        out_shape=[jax.ShapeDtypeStruct((T * ROW_TILE, LANES), F32),
                   jax.ShapeDtypeStruct((8, T), I32), jax.ShapeDtypeStruct((8, T), I32),
                   jax.ShapeDtypeStruct((T, LANES), F32),
                   jax.ShapeDtypeStruct((E, LANES), F32)],
        scratch_shapes=[pltpu.VMEM((E, LANES), F32)],
        compiler_params=_cparams(("arbitrary",)),
        name="router",
    )(h2, g, wr_t, br)


def _positions_kernel(ps_ref, e_ref, r_ref, pos_ref):
    e = e_ref[...]
    pos = r_ref[...]
    for x in range(N_EXPERTS):
        pos = pos + jnp.where(e == x, ps_ref[x], 0)
    for c in range(pos_ref.shape[0]):
        pos_ref[c] = pos[:, c * LANES:(c + 1) * LANES]


def _positions(pstart, e8, r8, tm=2048):
    T = e8.shape[1]
    grid_spec = pltpu.PrefetchScalarGridSpec(
        num_scalar_prefetch=1,
        grid=(T // tm,),
        in_specs=[pl.BlockSpec((8, tm), lambda i, ps: (0, i)), pl.BlockSpec((8, tm), lambda i, ps: (0, i))],
        out_specs=pl.BlockSpec((tm // LANES, 8, LANES), lambda i, ps: (i, 0, 0)),
    )
    return pl.pallas_call(
        _positions_kernel, grid_spec=grid_spec, out_shape=jax.ShapeDtypeStruct((T // LANES, 8, LANES), I32),
        compiler_params=_cparams(("arbitrary",)), name="positions",
    )(pstart, e8, r8)


def _row_dmas(tm, make_copy, wait):
    for c in range(tm // LANES):
        for k0 in range(0, EXPERT_TOPK, 2):
            def body(lane, carry, c=c, k0=k0):
                for k in (k0, k0 + 1):
                    cp = make_copy(c, k, lane)
                    cp.wait() if wait else cp.start(priority=k % 2)
                return carry
            lax.fori_loop(0, LANES, body, 0, unroll=4)


def _dispatch_kernel(zr_ref, pos_ref, xn_ref, xs_ref, zbuf, sem, zsem):
    tm = xn_ref.shape[0] // ROW_TILE

    @pl.when(pl.program_id(0) == 0)
    def _():
        zbuf[...] = jnp.zeros_like(zbuf)

        def zero_copy(e):
            first = pl.multiple_of(zr_ref[e] * ROW_TILE, MOE_BLOCK * ROW_TILE)
            return pltpu.make_async_copy(zbuf, xs_ref.at[pl.ds(first, MOE_BLOCK * ROW_TILE)], zsem)

        for e in range(2 * N_EXPERTS):
            @pl.when(zr_ref[e] >= 0)
            def _():
                zero_copy(e).start()
        for e in range(2 * N_EXPERTS):
            @pl.when(zr_ref[e] >= 0)
            def _():
                zero_copy(e).wait()

    row_copy = lambda c, k, lane: pltpu.make_async_copy(
        _tile(xn_ref, c * LANES + lane), _tile(xs_ref, _pos(pos_ref, c, k, lane)), sem)
    _row_dmas(tm, row_copy, wait=False)
    _row_dmas(tm, row_copy, wait=True)


def _dispatch(zrow, pos3, xn, rows_total, tm=1024):
    T = xn.shape[0] // ROW_TILE
    grid_spec = pltpu.PrefetchScalarGridSpec(
        num_scalar_prefetch=1,
        grid=(T // tm,),
        in_specs=[pl.BlockSpec((tm * 8,), lambda i, zr: (i,), memory_space=pltpu.SMEM),
                  pl.BlockSpec((tm * ROW_TILE, LANES), lambda i, zr: (i, 0))],
        out_specs=pl.BlockSpec(memory_space=pl.ANY),
        scratch_shapes=[pltpu.VMEM((MOE_BLOCK * ROW_TILE, LANES), F32), pltpu.SemaphoreType.DMA(()),
                        pltpu.SemaphoreType.DMA(())],
    )
    return pl.pallas_call(
        _dispatch_kernel,
        grid_spec=grid_spec,
        out_shape=jax.ShapeDtypeStruct((rows_total * ROW_TILE, LANES), F32),
        compiler_params=_cparams(("arbitrary",)),
        name="dispatch",
    )(zrow, pos3, xn)


def _experts_kernel(be_ref, nu_ref, xs_ref, wgu_ref, bgu_ref, wdn_ref, bdn_ref, ys_ref, wgu_bf, wdn_bf):
    i = pl.program_id(0)
    dff = wdn_ref.shape[1]

    @pl.when(i < nu_ref[0])
    def _():
        prev = be_ref[jnp.maximum(i - 1, 0)]

        @pl.when((i == 0) | (be_ref[i] != prev))
        def _():
            wgu_bf[...] = wgu_ref[0].astype(BF16)
            wdn_bf[...] = wdn_ref[0].astype(BF16)

        gu = _dot(_from_row_tiles(xs_ref, MOE_BLOCK).astype(BF16), wgu_bf[...]) + bgu_ref[0]
        gate = jnp.minimum(gu[:, :dff], SWIGLU_LIMIT)
        up = jnp.clip(gu[:, dff:], -SWIGLU_LIMIT, SWIGLU_LIMIT)
        hid = (up + 1.0) * gate * jax.nn.sigmoid(SWIGLU_ALPHA * gate)
        _to_row_tiles(ys_ref, _dot(hid.astype(BF16), wdn_bf[...]) + bdn_ref[0])

    @pl.when(i >= nu_ref[0])
    def _():
        ys_ref[...] = jnp.zeros_like(ys_ref)


def _experts(blk_expert, nused, xs, wgu, bgu, wdn, bdn):
    rows = xs.shape[0] // ROW_TILE
    E, D, F2 = wgu.shape
    dff = wdn.shape[1]
    nblk = rows // MOE_BLOCK
    grid_spec = pltpu.PrefetchScalarGridSpec(
        num_scalar_prefetch=2,
        grid=(nblk,),
        in_specs=[pl.BlockSpec((MOE_BLOCK * ROW_TILE, LANES), lambda i, be, nu: (jnp.minimum(i, nu[0] - 1), 0)),
                  pl.BlockSpec((1, D, F2), lambda i, be, nu: (be[i], 0, 0)),
                  pl.BlockSpec((1, 1, F2), lambda i, be, nu: (be[i], 0, 0)),
                  pl.BlockSpec((1, dff, D), lambda i, be, nu: (be[i], 0, 0)),
                  pl.BlockSpec((1, 1, D), lambda i, be, nu: (be[i], 0, 0))],
        out_specs=pl.BlockSpec((MOE_BLOCK * ROW_TILE, LANES), lambda i, be, nu: (i, 0)),
        scratch_shapes=[pltpu.VMEM((D, F2), BF16), pltpu.VMEM((dff, D), BF16)],
    )
    return pl.pallas_call(
        _experts_kernel,
        grid_spec=grid_spec,
        out_shape=jax.ShapeDtypeStruct((rows * ROW_TILE, LANES), F32),
        compiler_params=_cparams(("arbitrary",)),
        name="experts",
    )(blk_expert, nused, xs, wgu, bgu, wdn, bdn)


def _combine_kernel(pos_ref, nxt_ref, h_ref, gate_ref, gf_ref, ys_ref, o_ref, buf, sem):
    i = pl.program_id(0)
    tm = h_ref.shape[0]
    slot = i % 2

    def gather(p_ref, slot):
        first = slot * (EXPERT_TOPK * tm)
        return lambda c, k, lane: pltpu.make_async_copy(
            _tile(ys_ref, _pos(p_ref, c, k, lane)), _tile(buf, first + k * tm + c * LANES + lane), sem.at[slot])

    @pl.when(i == 0)
    def _():
        _row_dmas(tm, gather(pos_ref, slot), wait=False)

    @pl.when(i + 1 < pl.num_programs(0))
    def _():
        _row_dmas(tm, gather(nxt_ref, 1 - slot), wait=False)

    _row_dmas(tm, gather(pos_ref, slot), wait=True)
    gates = gate_ref[...]
    acc = h_ref[...]
    for k in range(EXPERT_TOPK):
        acc = acc + gates[:, k:k + 1] * _from_row_tiles(buf, tm, first=slot * (EXPERT_TOPK * tm) + k * tm)
    o_ref[...] = _rmsnorm(acc, gf_ref[...])


def _combine(pos3, h2, gates, gf, ys, tm=512):
    T, D = h2.shape
    return pl.pallas_call(
        _combine_kernel,
        grid=(T // tm,),
        in_specs=[pl.BlockSpec((tm * 8,), lambda i: (i,), memory_space=pltpu.SMEM),
                  pl.BlockSpec((tm * 8,), lambda i: (jnp.minimum(i + 1, T // tm - 1),), memory_space=pltpu.SMEM),
                  pl.BlockSpec((tm, D), lambda i: (i, 0)),
                  pl.BlockSpec((tm, LANES), lambda i: (i, 0)),
                  pl.BlockSpec((1, D), lambda i: (0, 0)),
                  pl.BlockSpec(memory_space=pl.ANY)],
        out_specs=pl.BlockSpec((tm, D), lambda i: (i, 0)),
        out_shape=jax.ShapeDtypeStruct((T, D), F32),
        scratch_shapes=[pltpu.VMEM((2 * EXPERT_TOPK * tm * ROW_TILE, LANES), F32), pltpu.SemaphoreType.DMA((2,))],
        compiler_params=_cparams(("arbitrary",)),
        name="combine",
    )(pos3, pos3, h2, gates, gf, ys)


def _pad_lanes(a, width=LANES, value=0.0):
    pad = [(0, 0)] * (a.ndim - 1) + [(0, width - a.shape[-1])]
    return jnp.pad(a, pad, constant_values=value)


def _layer(h2d, mem2d, batch, seq, mem_len, g_mix, w_in, a_re, a_im, log_dt, b_re, b_im, c_re, c_im, d_skip,
           w_glu, w_proj_a, w_out, g_xattn, g_mem, w_xq, w_xkv, w_xo, g_moe, w_router, b_router,
           w_gu, b_gu, w_dn, b_dn, g_final):
    T, D = h2d.shape
    aw = ATT_HEADS * ATT_HEAD_DIM
    G = a_re.shape[0]
    sw = G * SSM_GROUP

    wq, wk, wv, wu, wga, wgb = jnp.split(w_in, [aw, 2 * aw, 3 * aw, 3 * aw + sw, 3 * aw + sw + D], axis=1)
    wg_t = jnp.concatenate([wga, wgb, wu], axis=1).T.astype(BF16)
    wqv_t = jnp.concatenate([wq * (ATT_HEAD_DIM ** -0.5), wv], axis=1).T.astype(BF16)
    wk_pad = _pad_lanes(wk.reshape(D, ATT_HEADS, ATT_HEAD_DIM)).reshape(D, ATT_HEADS * LANES).astype(BF16)

    zt, q4, v4, kpad, km = _in_proj(h2d, g_mix.reshape(1, D), wg_t, wqv_t, wk_pad)
    nb = seq // MOBA_BLOCK
    att4 = _moba(q4, v4, kpad.reshape(T // MOBA_BLOCK, MOBA_BLOCK, ATT_HEADS * LANES),
                 km.reshape(batch, nb, ATT_HEADS * LANES), batch)

    row = lambda a, v=0.0: _pad_lanes(a, value=v).reshape(G, 1, LANES)
    col = lambda a, v=0.0: _pad_lanes(a, value=v).reshape(G, LANES, 1)
    ssm_params = (
        row(a_re, -1.0), row(a_im), log_dt.reshape(G, 1, 1), col(a_re, -1.0), col(a_im),
        _pad_lanes(jnp.swapaxes(b_re, 1, 2)), _pad_lanes(jnp.swapaxes(b_im, 1, 2)),
        _pad_lanes(c_re), _pad_lanes(c_im),
        _pad_lanes(c_re)[..., None], _pad_lanes(c_im)[..., None],
        d_skip.reshape(G, SSM_GROUP, 1, 1),
    )
    u3 = zt[2 * D:].reshape(sw, T // SSM_CHUNK, SSM_CHUNK)
    ys3 = _ssm(u3, 0, ssm_params, batch)
    h1 = _mixer_out(zt, ys3.reshape(sw, T), att4, h2d, w_glu.T.astype(BF16), w_proj_a.T.astype(BF16),
                    w_out.astype(BF16))

    kv = _memkv(mem2d, g_mem.reshape(1, D), w_xkv.astype(BF16), mem_len)
    h2 = _xattn(h1, g_xattn.reshape(1, D), w_xq.astype(BF16), kv, w_xo.astype(BF16), seq, mem_len)

    xn, e8, r8, gates, cnt = _router(h2, g_moe.reshape(1, D), w_router.T, b_router.reshape(N_EXPERTS, 1))
    counts = cnt[:, 0].astype(I32)
    nblk_e = (counts + MOE_BLOCK - 1) // MOE_BLOCK
    bends = jnp.cumsum(nblk_e)
    pstart = ((bends - nblk_e) * MOE_BLOCK).astype(I32)
    nblk = (T * EXPERT_TOPK) // MOE_BLOCK + N_EXPERTS
    blk_expert = jnp.minimum(jnp.sum(bends[None, :] <= jnp.arange(nblk, dtype=I32)[:, None], axis=1),
                             N_EXPERTS - 1).astype(I32)
    nused = bends[-1:].astype(I32)
    blk_expert = jnp.where(jnp.arange(nblk) < nused[0], blk_expert, blk_expert[jnp.maximum(nused[0] - 1, 0)])

    tail = nblk - 1 - jnp.arange(N_EXPERTS, dtype=I32)
    zrow = jnp.concatenate([jnp.where(nblk_e > 0, (bends - 1) * MOE_BLOCK, -1),
                            jnp.where(tail >= nused[0], tail * MOE_BLOCK, -1)]).astype(I32)
    pos3 = _positions(pstart, e8, r8).reshape(-1)
    xs = _dispatch(zrow, pos3, xn, nblk * MOE_BLOCK)
    ys = _experts(blk_expert, nused, xs, w_gu, b_gu.reshape(N_EXPERTS, 1, -1), w_dn, b_dn.reshape(N_EXPERTS, 1, -1))
    return _combine(pos3, h2, gates, g_final.reshape(1, D), ys)


def kernel(x, mem, g_mix, w_in, a_re, a_im, log_dt, b_re, b_im, c_re, c_im, d_skip, w_glu, w_proj_a, w_out,
           g_xattn, g_mem, w_xq, w_xkv, w_xo, g_moe, w_router, b_router, w_gu, b_gu, w_dn, b_dn, g_final):
    B, S, D = x.shape
    M = mem.shape[1]
    assert g_mix.shape[0] == 1, "single layer"
    out = _layer(x.reshape(B * S, D), mem.reshape(B * M, D), B, S, M,
                 g_mix[0], w_in[0], a_re[0], a_im[0], log_dt[0], b_re[0], b_im[0], c_re[0], c_im[0], d_skip[0],
                 w_glu[0], w_proj_a[0], w_out[0], g_xattn[0], g_mem[0], w_xq[0], w_xkv[0], w_xo[0],
                 g_moe[0], w_router[0], b_router[0], w_gu[0], b_gu[0], w_dn[0], b_dn[0], g_final)
    return out.reshape(B, S, D)
```
